```python
import math
import jax
import jax.numpy as jnp
from jax import lax
import numpy as np

D_MODEL = 1024
BATCH = 4
SEQ = 8192
DEPTH = 4

CHUNK = 64
N_EVEN = (DEPTH + 1) // 2
N_ODD = DEPTH // 2
RMS_EPS = 1e-6

DIFF_HEADS = 4
DIFF_HEAD_DIM = 64
DIFF_V_DIM = 2 * DIFF_HEAD_DIM
DIFF_WIDTH = DIFF_HEADS * DIFF_V_DIM
Q_BLOCK = 128

POOL_WINDOWS = (2, 4, 8, 16)
POOL_GROUPS = len(POOL_WINDOWS)
POOL_WIDTH = D_MODEL // 2
POOL_GROUP_DIM = POOL_WIDTH // POOL_GROUPS
AB_IN_WIDTH = 3 * DIFF_WIDTH + POOL_WIDTH
AB_OUT_IN = DIFF_WIDTH + POOL_WIDTH

GLA_HEADS = 4
GLA_KEY_DIM = D_MODEL // 2
GLA_VAL_DIM = D_MODEL
GLA_HK = GLA_KEY_DIM // GLA_HEADS
GLA_HV = GLA_VAL_DIM // GLA_HEADS
GLA_GATE_RANK = 16
GLA_GATE_TEMP = 16.0
GLA_IN_WIDTH = 2 * GLA_KEY_DIM + 2 * GLA_VAL_DIM + GLA_GATE_RANK

D_FF = 4 * D_MODEL

kernel_name = 'hybrid_diffattn_pool_gla_encoder'


def _rmsnorm(x, gain):
    xf = x.astype(jnp.float32)
    y = xf * lax.rsqrt(jnp.mean(xf * xf, axis=-1, keepdims=True) + RMS_EPS)
    return (y * gain.astype(jnp.float32)).astype(x.dtype)


def _lambda_init(layer_idx):
    return 0.8 - 0.6 * math.exp(-0.3 * layer_idx)


def _diff_attention(q, k, v, lam_params, subln_gain, layer_idx):
    bsz, seq, _ = q.shape
    q = q.reshape(bsz, seq, DIFF_HEADS, 2, DIFF_HEAD_DIM)
    k = k.reshape(bsz, seq, DIFF_HEADS, 2, DIFF_HEAD_DIM)
    q1 = q[:, :, :, 0].transpose(0, 2, 1, 3)
    q2 = q[:, :, :, 1].transpose(0, 2, 1, 3)
    k1 = k[:, :, :, 0].transpose(0, 2, 1, 3)
    k2 = k[:, :, :, 1].transpose(0, 2, 1, 3)
    v = v.reshape(bsz, seq, DIFF_HEADS, DIFF_V_DIM).transpose(0, 2, 1, 3)
    lam_p = lam_params.astype(jnp.float32)
    lam_init = _lambda_init(layer_idx)
    lam = jnp.exp(jnp.sum(lam_p[0] * lam_p[1])) - jnp.exp(jnp.sum(lam_p[2] * lam_p[3])) + lam_init
    n_blk = seq // Q_BLOCK

    def to_blocks(t):
        return t.reshape(bsz, DIFF_HEADS, n_blk, Q_BLOCK, DIFF_HEAD_DIM).transpose(2, 0, 1, 3, 4)

    key_chunk = jnp.arange(seq) // CHUNK
    scale = DIFF_HEAD_DIM ** -0.5

    def one_block(args):
        q1_b, q2_b, blk = args
        query_chunk = (blk * Q_BLOCK + jnp.arange(Q_BLOCK)) // CHUNK
        mask = key_chunk[None, :] <= query_chunk[:, None]
        s1 = jnp.einsum('bhqd,bhkd->bhqk', q1_b, k1).astype(jnp.float32) * scale
        s2 = jnp.einsum('bhqd,bhkd->bhqk', q2_b, k2).astype(jnp.float32) * scale
        p1 = jax.nn.softmax(jnp.where(mask, s1, -jnp.inf), axis=-1)
        p2 = jax.nn.softmax(jnp.where(mask, s2, -jnp.inf), axis=-1)
        p = p1 - lam * p2
        return jnp.einsum('bhqk,bhkv->bhqv', p.astype(v.dtype), v)

    o = lax.map(one_block, (to_blocks(q1), to_blocks(q2), jnp.arange(n_blk)))
    o = o.transpose(1, 2, 0, 3, 4).reshape(bsz, DIFF_HEADS, seq, DIFF_V_DIM)
    o = _rmsnorm(o, subln_gain) * (1.0 - lam_init)
    return o.transpose(0, 2, 1, 3).reshape(bsz, seq, DIFF_WIDTH)


def _pool_mixer(u, pool_w, pool_scale):
    bsz, seq, _ = u.shape
    u = u.reshape(bsz, seq, POOL_GROUPS, POOL_GROUP_DIM)
    pos = jnp.arange(seq)
    outs = []
    for g, w in enumerate(POOL_WINDOWS):
        ug = u[:, :, g].astype(jnp.float32)
        cs = jnp.cumsum(ug, axis=1)
        lagged = jnp.pad(cs, ((0, 0), (w, 0), (0, 0)))[:, :seq]
        count = jnp.minimum(pos + 1, w).astype(jnp.float32)
        outs.append((cs - lagged) / count[None, :, None] - ug)
    r = jnp.stack(outs, axis=2).astype(u.dtype)
    y = jnp.einsum('bsgc,gcd->bsgd', r, pool_w) * pool_scale.reshape(POOL_GROUPS, POOL_GROUP_DIM)
    return y.reshape(bsz, seq, POOL_WIDTH)


def _gla_mixer(h, w_in, w_gk_up, b_gk, norm_gain, w_out):
    bsz, seq, _ = h.shape
    n_chunk = seq // CHUNK
    proj = h @ w_in
    q, k, v, g_out, gk_low = jnp.split(
        proj,
        [GLA_KEY_DIM, 2 * GLA_KEY_DIM, 2 * GLA_KEY_DIM + GLA_VAL_DIM, 2 * GLA_KEY_DIM + 2 * GLA_VAL_DIM],
        axis=-1)
    log_a = jax.nn.log_sigmoid((gk_low @ w_gk_up + b_gk).astype(jnp.float32)) / GLA_GATE_TEMP

    def to_chunks(t, d):
        return t.reshape(bsz, n_chunk, CHUNK, GLA_HEADS, d).transpose(1, 0, 3, 2, 4)

    qc = to_chunks(q * (GLA_HK ** -0.5), GLA_HK)
    kc = to_chunks(k, GLA_HK)
    vc = to_chunks(v, GLA_HV)
    gc = to_chunks(log_a, GLA_HK)
    causal = jnp.tril(jnp.ones((CHUNK, CHUNK), dtype=bool))

    def step(state, inp):
        qi, ki, vi, gi = inp
        qi = qi.astype(jnp.float32)
        ki = ki.astype(jnp.float32)
        vi = vi.astype(jnp.float32)
        b = jnp.cumsum(gi, axis=2)
        o_inter = jnp.einsum('bhlk,bhkv->bhlv', qi * jnp.exp(b), state)
        diff = b[:, :, :, None, :] - b[:, :, None, :, :]
        decay = jnp.exp(jnp.where(causal[:, :, None], diff, -jnp.inf))
        attn = jnp.einsum('bhijk,bhjk->bhij', qi[:, :, :, None, :] * decay, ki)
        o_intra = jnp.einsum('bhij,bhjv->bhiv', attn, vi)
        b_last = b[:, :, -1:, :]
        new_state = (jnp.exp(b_last[:, :, 0, :])[..., None] * state
                     + jnp.einsum('bhlk,bhlv->bhkv', ki * jnp.exp(b_last - b), vi))
        return new_state, o_inter + o_intra

    state0 = jnp.zeros((bsz, GLA_HEADS, GLA_HK, GLA_HV), jnp.float32)
    _, o = lax.scan(step, state0, (qc, kc, vc, gc))
    o = o.transpose(1, 0, 3, 2, 4).reshape(bsz, seq, GLA_HEADS, GLA_HV).astype(h.dtype)
    gate = jax.nn.silu(g_out).reshape(bsz, seq, GLA_HEADS, GLA_HV)
    o = _rmsnorm(o, norm_gain) * gate
    return o.reshape(bsz, seq, GLA_VAL_DIM) @ w_out


def setup_inputs(seed: int = 0) -> dict:
    key = jax.random.key(seed)
    ks = jax.random.split(key, 17)
    f32 = jnp.float32

    def nrm(k, shape, scale):
        return jax.random.normal(k, shape, f32) * scale

    return {
        'x': nrm(ks[0], (BATCH, SEQ, D_MODEL), 1.0),
        'norm_mix': 1.0 + nrm(ks[1], (DEPTH, D_MODEL), 0.02),
        'norm_ffn': 1.0 + nrm(ks[2], (DEPTH, D_MODEL), 0.02),
        'norm_final': 1.0 + nrm(ks[3], (D_MODEL,), 0.02),
        'ab_w_in': nrm(ks[4], (N_EVEN, D_MODEL, AB_IN_WIDTH), D_MODEL ** -0.5),
        'ab_lambda': nrm(ks[5], (N_EVEN, 4, DIFF_HEAD_DIM), 0.1),
        'ab_subln': 1.0 + nrm(ks[6], (N_EVEN, DIFF_V_DIM), 0.02),
        'pool_w': nrm(ks[7], (N_EVEN, POOL_GROUPS, POOL_GROUP_DIM, POOL_GROUP_DIM), POOL_GROUP_DIM ** -0.5),
        'pool_scale': 1.0 + nrm(ks[8], (N_EVEN, POOL_WIDTH), 0.1),
        'ab_w_out': nrm(ks[9], (N_EVEN, AB_OUT_IN, D_MODEL), AB_OUT_IN ** -0.5),
        'gla_w_in': nrm(ks[10], (N_ODD, D_MODEL, GLA_IN_WIDTH), D_MODEL ** -0.5),
        'gla_w_gk_up': nrm(ks[11], (N_ODD, GLA_GATE_RANK, GLA_KEY_DIM), GLA_GATE_RANK ** -0.5),
        'gla_b_gk': nrm(ks[12], (N_ODD, GLA_KEY_DIM), 0.1),
        'gla_norm': 1.0 + nrm(ks[13], (N_ODD, GLA_HEADS, GLA_HV), 0.02),
        'gla_w_out': nrm(ks[14], (N_ODD, GLA_VAL_DIM, D_MODEL), GLA_VAL_DIM ** -0.5),
        'ffn_w1': nrm(ks[15], (DEPTH, D_MODEL, D_FF), D_MODEL ** -0.5),
        'ffn_w2': nrm(ks[16], (DEPTH, D_FF, D_MODEL), D_FF ** -0.5),
    }


def reference(x, norm_mix, norm_ffn, norm_final, ab_w_in, ab_lambda, ab_subln, pool_w, pool_scale,
              ab_w_out, gla_w_in, gla_w_gk_up, gla_b_gk, gla_norm, gla_w_out, ffn_w1, ffn_w2):
    for i in range(DEPTH):
        h = _rmsnorm(x, norm_mix[i])
        if i % 2 == 0:
            e = i // 2
            proj = h @ ab_w_in[e]
            q, k, v, u = jnp.split(proj, [DIFF_WIDTH, 2 * DIFF_WIDTH, 3 * DIFF_WIDTH], axis=-1)
            a_out = _diff_attention(q, k, v, ab_lambda[e], ab_subln[e], i)
            b_out = _pool_mixer(u, pool_w[e], pool_scale[e])
            x = x + jnp.concatenate([a_out, b_out], axis=-1) @ ab_w_out[e]
        else:
            o = i // 2
            x = x + _gla_mixer(h, gla_w_in[o], gla_w_gk_up[o], gla_b_gk[o], gla_norm[o], gla_w_out[o])
        h = _rmsnorm(x, norm_ffn[i])
        x = x + jnp.square(jax.nn.relu(h @ ffn_w1[i])) @ ffn_w2[i]
    return _rmsnorm(x, norm_final)
```

```python
import functools
import math

import jax
import jax.numpy as jnp
from jax import lax
from jax.experimental import pallas as pl
from jax.experimental.pallas import tpu as pltpu

F32 = jnp.float32
BF16 = jnp.bfloat16

RMS_EPS = 1e-6
CHUNK = 64

DIFF_HEADS = 4
DIFF_HEAD_DIM = 64
DIFF_V_DIM = 2 * DIFF_HEAD_DIM
DIFF_WIDTH = DIFF_HEADS * DIFF_V_DIM
POOL_WINDOWS = (2, 4, 8, 16)
POOL_GROUP_DIM = 128
POOL_WIDTH = POOL_GROUP_DIM * len(POOL_WINDOWS)
POOL_HALO = 16

GLA_HEADS = 4
GLA_HK = 128
GLA_HV = 256
GLA_KEY_DIM = GLA_HEADS * GLA_HK
GLA_VAL_DIM = GLA_HEADS * GLA_HV
GLA_GATE_RANK = 16
GLA_GATE_TEMP = 16.0
GLA_GATE_PAD = 128
GLA_SUB = 16
GLA_EXP_CAP = 60.0

VMEM_LIMIT_BYTES = 56 * 1024 * 1024
TOKEN_TILE = 512
ATTN_TILE = 512
GLA_TILE = 512
ATTN_V_ROWS = DIFF_V_DIM + 16

LOG2E = 1.4426950408889634
NEG_BIG = -1e30


def _dot(a, b):
    return jnp.dot(a, b, preferred_element_type=F32)


def _dot_nt(a, b):
    return lax.dot_general(a, b, (((1,), (1,)), ((), ())), preferred_element_type=F32)


def _dot_tn(a, b):
    return lax.dot_general(a, b, (((0,), (0,)), ((), ())), preferred_element_type=F32)


def _rms(x, gain):
    return x * lax.rsqrt(jnp.mean(x * x, axis=-1, keepdims=True) + RMS_EPS) * gain


def _const_spec(shape):
    nd = len(shape)
    return pl.BlockSpec(shape, lambda *_: (0,) * nd, pipeline_mode=pl.Buffered(1))


def _params(sem):
    return pltpu.CompilerParams(dimension_semantics=sem, vmem_limit_bytes=VMEM_LIMIT_BYTES)


def _ab_in_kernel(x_ref, g_ref, w_ref, q_ref, k_ref, v_ref, u_ref):
    hb = _rms(x_ref[...], g_ref[...]).astype(BF16)
    w = DIFF_WIDTH
    q_ref[...] = (_dot(hb, w_ref[:, 0:w]) * (DIFF_HEAD_DIM ** -0.5 * LOG2E)).astype(BF16)
    k_ref[...] = _dot(hb, w_ref[:, w:2 * w]).astype(BF16)
    v_ref[...] = _dot(hb, w_ref[:, 2 * w:3 * w]).astype(BF16)
    u_ref[...] = _dot(hb, w_ref[:, 3 * w:3 * w + POOL_WIDTH])


def _ab_in_proj(x2, gain, w_in):
    n, d = x2.shape
    tm = TOKEN_TILE
    tok = lambda width: pl.BlockSpec((tm, width), lambda i: (i, 0))
    return pl.pallas_call(
        _ab_in_kernel,
        grid=(n // tm,),
        in_specs=[tok(d), _const_spec((1, d)), _const_spec(w_in.shape)],
        out_specs=[tok(DIFF_WIDTH), tok(DIFF_WIDTH), tok(DIFF_WIDTH), tok(POOL_WIDTH)],
        out_shape=[jax.ShapeDtypeStruct((n, DIFF_WIDTH), BF16)] * 3 + [jax.ShapeDtypeStruct((n, POOL_WIDTH), F32)],
        compiler_params=_params(("parallel",)),
        name="ab_in_proj",
    )(x2, gain.reshape(1, d), w_in)


def _attn_kernel(q_ref, k_ref, vt_ref, lam_ref, gain_ref, o_ref, m_sc, acc_sc, *, tile, lam_init):
    qi = pl.program_id(2)
    q = q_ref[0]
    lane = lax.broadcasted_iota(jnp.int32, q.shape, 1)
    zero = jnp.zeros_like(q)
    qz = jnp.concatenate([jnp.where(lane < DIFF_HEAD_DIM, q, zero), jnp.where(lane >= DIFF_HEAD_DIM, q, zero)], axis=0)

    m_sc[...] = jnp.full(m_sc.shape, NEG_BIG, F32)
    acc_sc[...] = jnp.zeros(acc_sc.shape, F32)

    def block(i, mask):
        kblk = k_ref[0, pl.ds(pl.multiple_of(i * tile, tile), tile), :]
        s = _dot_nt(kblk, qz)
        if mask is not None:
            s = jnp.where(mask, s, NEG_BIG)
        m_old = m_sc[...]
        m_new = jnp.maximum(m_old, jnp.max(s, axis=0, keepdims=True))
        p = jnp.exp2(s - m_new).astype(BF16)
        acc_sc[...] = acc_sc[...] * jnp.exp2(m_old - m_new) + _dot(vt_ref[0, i], p)
        m_sc[...] = m_new

    def body(i, carry):
        block(i, None)
        return carry

    lax.fori_loop(0, qi, body, 0)

    kc = lax.broadcasted_iota(jnp.int32, (tile, 2 * tile), 0) // CHUNK
    qcol = lax.broadcasted_iota(jnp.int32, (tile, 2 * tile), 1)
    qc = jnp.where(qcol >= tile, qcol - tile, qcol) // CHUNK
    block(qi, kc <= qc)

    acc = acc_sc[...]
    o = acc[:DIFF_V_DIM, :] / acc[DIFF_V_DIM:DIFF_V_DIM + 1, :]
    lp = lam_ref[...]
    lam = (jnp.exp(jnp.sum(lp[0:1] * lp[1:2], axis=1, keepdims=True))
           - jnp.exp(jnp.sum(lp[2:3] * lp[3:4], axis=1, keepdims=True)) + lam_init)
    d = o[:, :tile] - lam * o[:, tile:]
    y = d * lax.rsqrt(jnp.mean(d * d, axis=0, keepdims=True) + RMS_EPS) * gain_ref[...] * (1.0 - lam_init)
    o_ref[0] = y.T.astype(BF16)


def _diff_attention(q, k, v, lam_params, subln_gain, layer_idx, bsz, seq):
    tile = ATTN_TILE
    nblk = seq // tile
    lam_init = 0.8 - 0.6 * math.exp(-0.3 * layer_idx)
    q3 = q.reshape(bsz, seq, DIFF_WIDTH)
    k3 = k.reshape(bsz, seq, DIFF_WIDTH)
    vt = v.reshape(bsz, nblk, tile, DIFF_HEADS, DIFF_V_DIM).transpose(0, 3, 1, 4, 2)
    extra = jnp.zeros((bsz, DIFF_HEADS, nblk, ATTN_V_ROWS - DIFF_V_DIM, tile), BF16).at[:, :, :, 0, :].set(1.0)
    vt = jnp.concatenate([vt, extra], axis=3).reshape(bsz * DIFF_HEADS, nblk, ATTN_V_ROWS, tile)
    gain_b = jnp.broadcast_to(subln_gain.astype(F32).reshape(DIFF_V_DIM, 1), (DIFF_V_DIM, tile))
    out = pl.pallas_call(
        functools.partial(_attn_kernel, tile=tile, lam_init=lam_init),
        grid=(bsz, DIFF_HEADS, nblk),
        in_specs=[
            pl.BlockSpec((1, tile, DIFF_V_DIM), lambda b, h, i: (b, i, h)),
            pl.BlockSpec((1, seq, DIFF_V_DIM), lambda b, h, i: (b, 0, h)),
            pl.BlockSpec((1, nblk, ATTN_V_ROWS, tile), lambda b, h, i: (b * DIFF_HEADS + h, 0, 0, 0)),
            _const_spec((4, DIFF_HEAD_DIM)),
            _const_spec((DIFF_V_DIM, tile)),
        ],
        out_specs=pl.BlockSpec((1, tile, DIFF_V_DIM), lambda b, h, i: (b, i, h)),
        out_shape=jax.ShapeDtypeStruct((bsz, seq, DIFF_WIDTH), BF16),
        scratch_shapes=[pltpu.VMEM((1, 2 * tile), F32), pltpu.VMEM((ATTN_V_ROWS, 2 * tile), F32)],
        compiler_params=_params(("parallel", "parallel", "arbitrary")),
        name="diff_attention",
    )(q3, k3, vt, lam_params.astype(F32), gain_b)
    return out.reshape(bsz * seq, DIFF_WIDTH)


def _gla_in_kernel(x_ref, g_ref, w_ref, wup_ref, bgk_ref, q_ref, k_ref, v_ref, go_ref, la_ref):
    hb = _rms(x_ref[...], g_ref[...]).astype(BF16)
    kd, vd = GLA_KEY_DIM, GLA_VAL_DIM
    q_ref[...] = (_dot(hb, w_ref[:, 0:kd]) * (GLA_HK ** -0.5)).astype(BF16)
    k_ref[...] = _dot(hb, w_ref[:, kd:2 * kd]).astype(BF16)
    v_ref[...] = _dot(hb, w_ref[:, 2 * kd:2 * kd + vd]).astype(BF16)
    go_ref[...] = _dot(hb, w_ref[:, 2 * kd + vd:2 * kd + 2 * vd]).astype(BF16)
    low = _dot(hb, w_ref[:, 2 * kd + 2 * vd:2 * kd + 2 * vd + GLA_GATE_PAD])
    z = _dot(low.astype(BF16), wup_ref[...]) + bgk_ref[...]
    la_ref[...] = (jnp.minimum(z, 0.0) - jnp.log(1.0 + jnp.exp(-jnp.abs(z)))) * (1.0 / GLA_GATE_TEMP)


def _gla_in_proj(x2, gain, w_in_pad, w_up_pad, b_gk):
    n, d = x2.shape
    tm = TOKEN_TILE
    tok = lambda width: pl.BlockSpec((tm, width), lambda i: (i, 0))
    return pl.pallas_call(
        _gla_in_kernel,
        grid=(n // tm,),
        in_specs=[tok(d), _const_spec((1, d)), _const_spec(w_in_pad.shape), _const_spec(w_up_pad.shape),
                  _const_spec((1, GLA_KEY_DIM))],
        out_specs=[tok(GLA_KEY_DIM), tok(GLA_KEY_DIM), tok(GLA_VAL_DIM), tok(GLA_VAL_DIM), tok(GLA_KEY_DIM)],
        out_shape=[jax.ShapeDtypeStruct((n, GLA_KEY_DIM), BF16), jax.ShapeDtypeStruct((n, GLA_KEY_DIM), BF16),
                   jax.ShapeDtypeStruct((n, GLA_VAL_DIM), BF16), jax.ShapeDtypeStruct((n, GLA_VAL_DIM), BF16),
                   jax.ShapeDtypeStruct((n, GLA_KEY_DIM), F32)],
        compiler_params=_params(("parallel",)),
        name="gla_in_proj",
    )(x2, gain.reshape(1, d), w_in_pad, w_up_pad, b_gk.reshape(1, GLA_KEY_DIM))


def _gla_core_kernel(q_ref, k_ref, v_ref, go_ref, la_ref, gn_ref, o_ref, st_ref, *, tile):
    @pl.when(pl.program_id(2) == 0)
    def _():
        st_ref[...] = jnp.zeros(st_ref.shape, F32)

    rin = lax.broadcasted_iota(jnp.int32, (tile, GLA_HK), 0) & (CHUNK - 1)
    b = la_ref[0]
    sh = 1
    while sh < CHUNK:
        b = b + jnp.where(rin >= sh, pltpu.roll(b, sh, axis=0), 0.0)
        sh *= 2

    q = q_ref[0].astype(F32)
    k = k_ref[0].astype(F32)
    qe = (q * jnp.exp(b)).astype(BF16)
    causal = (lax.broadcasted_iota(jnp.int32, (CHUNK, CHUNK), 0)
              >= lax.broadcasted_iota(jnp.int32, (CHUNK, CHUNK), 1))
    gn = gn_ref[0]
    st = st_ref[...]

    for c in range(tile // CHUNK):
        r0 = c * CHUNK
        bc = b[r0:r0 + CHUNK]
        qc = q[r0:r0 + CHUNK]
        kc = k[r0:r0 + CHUNK]
        b_last = bc[CHUNK - 1:CHUNK]
        kt = (kc * jnp.exp(b_last - bc)).astype(BF16)
        rows = []
        for a in range(CHUNK // GLA_SUB):
            s0 = a * GLA_SUB
            bref = bc[s0:s0 + 1]
            qa = (qc[s0:s0 + GLA_SUB] * jnp.exp(bc[s0:s0 + GLA_SUB] - bref)).astype(BF16)
            ka = (kc * jnp.exp(jnp.minimum(bref - bc, GLA_EXP_CAP))).astype(BF16)
            rows.append(_dot_nt(qa, ka))
        attn = jnp.where(causal, jnp.concatenate(rows, axis=0), 0.0).astype(BF16)
        vc = v_ref[0, r0:r0 + CHUNK, :]
        o = _dot_nt(qe[r0:r0 + CHUNK], st.astype(BF16)) + _dot(attn, vc)
        st = st * jnp.exp(b_last) + _dot_tn(vc, kt)
        g = go_ref[0, r0:r0 + CHUNK, :].astype(F32)
        y = o * lax.rsqrt(jnp.mean(o * o, axis=-1, keepdims=True) + RMS_EPS) * gn
        o_ref[0, r0:r0 + CHUNK, :] = (y * (g / (1.0 + jnp.exp(-g)))).astype(BF16)

    st_ref[...] = st


def _gla_core(q, k, v, go, la, norm_gain, bsz, seq):
    tile = GLA_TILE
    kspec = pl.BlockSpec((1, tile, GLA_HK), lambda b, h, t: (b, t, h))
    vspec = pl.BlockSpec((1, tile, GLA_HV), lambda b, h, t: (b, t, h))
    out = pl.pallas_call(
        functools.partial(_gla_core_kernel, tile=tile),
        grid=(bsz, GLA_HEADS, seq // tile),
        in_specs=[kspec, kspec, vspec, vspec, kspec,
                  pl.BlockSpec((1, 1, GLA_HV), lambda b, h, t: (h, 0, 0))],
        out_specs=vspec,
        out_shape=jax.ShapeDtypeStruct((bsz, seq, GLA_VAL_DIM), BF16),
        scratch_shapes=[pltpu.VMEM((GLA_HV, GLA_HK), F32)],
        compiler_params=_params(("parallel", "parallel", "arbitrary")),
        name="gla_core",
    )(q.reshape(bsz, seq, GLA_KEY_DIM), k.reshape(bsz, seq, GLA_KEY_DIM), v.reshape(bsz, seq, GLA_VAL_DIM),
      go.reshape(bsz, seq, GLA_VAL_DIM), la.reshape(bsz, seq, GLA_KEY_DIM),
      norm_gain.astype(F32).reshape(GLA_HEADS, 1, GLA_HV))
    return out.reshape(bsz * seq, GLA_VAL_DIM)


def _pool_mixer(u, halo, first_tile_of_seq, tile_row0, pw_ref, ps_ref):
    tm = u.shape[0]
    halo = jnp.where(first_tile_of_seq, 0.0, halo)
    ext = jnp.concatenate([halo, u], axis=0)
    pos1 = lax.broadcasted_iota(jnp.int32, (tm, POOL_GROUP_DIM), 0) + (tile_row0 + 1)
    rs = []
    for g, w in enumerate(POOL_WINDOWS):
        cols = slice(g * POOL_GROUP_DIM, (g + 1) * POOL_GROUP_DIM)
        e = ext[:, cols]
        sh = 1
        while sh < w:
            e = e + pltpu.roll(e, sh, axis=0)
            sh *= 2
        cnt = jnp.minimum(pos1, w).astype(F32)
        rs.append((e[POOL_HALO:] / cnt - u[:, cols]).astype(BF16))
    ys = []
    for pair in range(2):
        r2 = jnp.concatenate(rs[2 * pair:2 * pair + 2], axis=1)
        ys.append(_dot(r2, pw_ref[pair]))
    return (jnp.concatenate(ys, axis=1) * ps_ref[...]).astype(BF16)


def _ffn(x, gf_ref, w1_ref, w2_ref):
    hb = _rms(x, gf_ref[...]).astype(BF16)
    d_ff = w1_ref.shape[1]
    step = 1024
    acc = x
    for c0 in range(0, d_ff, step):
        mid = jnp.maximum(_dot(hb, w1_ref[:, c0:c0 + step]), 0.0)
        acc = acc + _dot((mid * mid).astype(BF16), w2_ref[c0:c0 + step, :])
    return acc


def _pool_concat_kernel(a_ref, u_ref, uh_ref, pw_ref, ps_ref, o_ref, *, tiles_per_seq):
    tm = u_ref.shape[0]
    it = pl.program_id(0) % tiles_per_seq
    o_ref[:, 0:DIFF_WIDTH] = a_ref[...]
    o_ref[:, DIFF_WIDTH:] = _pool_mixer(u_ref[...], uh_ref[...], it == 0, it * tm, pw_ref, ps_ref)


def _mix_ffn_kernel(x_ref, y_ref, wo_ref, gf_ref, w1_ref, w2_ref, gl_ref, o_ref, *, final):
    x = x_ref[...] + _dot(y_ref[...], wo_ref[...])
    y = _ffn(x, gf_ref, w1_ref, w2_ref)
    o_ref[...] = _rms(y, gl_ref[...]) if final else y


def _pool_concat(a, u, pw_bd, pool_scale, seq):
    n = a.shape[0]
    tm = TOKEN_TILE
    tok = lambda width: pl.BlockSpec((tm, width), lambda i: (i, 0))
    halo_blocks = tm // POOL_HALO
    return pl.pallas_call(
        functools.partial(_pool_concat_kernel, tiles_per_seq=seq // tm),
        grid=(n // tm,),
        in_specs=[tok(DIFF_WIDTH), tok(POOL_WIDTH),
                  pl.BlockSpec((POOL_HALO, POOL_WIDTH), lambda i: (jnp.maximum(i * halo_blocks - 1, 0), 0)),
                  _const_spec(pw_bd.shape), _const_spec((1, POOL_WIDTH))],
        out_specs=tok(DIFF_WIDTH + POOL_WIDTH),
        out_shape=jax.ShapeDtypeStruct((n, DIFF_WIDTH + POOL_WIDTH), BF16),
        compiler_params=_params(("parallel",)),
        name="pool_concat",
    )(a, u, u, pw_bd, pool_scale.astype(F32).reshape(1, POOL_WIDTH))


def _mix_ffn(x2, y, w_out, gain_ffn, w1, w2, gain_last, final):
    n, d = x2.shape
    tm = TOKEN_TILE
    tok = lambda width: pl.BlockSpec((tm, width), lambda i: (i, 0))
    return pl.pallas_call(
        functools.partial(_mix_ffn_kernel, final=final),
        grid=(n // tm,),
        in_specs=[tok(d), tok(y.shape[1]), _const_spec(w_out.shape), _const_spec((1, d)), _const_spec(w1.shape),
                  _const_spec(w2.shape), _const_spec((1, d))],
        out_specs=tok(d),
        out_shape=jax.ShapeDtypeStruct((n, d), F32),
        compiler_params=_params(("parallel",)),
        name="mix_ffn",
    )(x2, y, w_out, gain_ffn.reshape(1, d), w1, w2, gain_last.reshape(1, d))


def _pool_block_diag(pool_w):
    g = POOL_GROUP_DIM
    z = jnp.zeros((g, g), pool_w.dtype)
    pairs = [jnp.block([[pool_w[2 * p], z], [z, pool_w[2 * p + 1]]]) for p in range(2)]
    return jnp.stack(pairs).astype(BF16)


def kernel(x, norm_mix, norm_ffn, norm_final, ab_w_in, ab_lambda, ab_subln, pool_w, pool_scale, ab_w_out, gla_w_in,
           gla_w_gk_up, gla_b_gk, gla_norm, gla_w_out, ffn_w1, ffn_w2):
    bsz, seq, d = x.shape
    depth = norm_mix.shape[0]
    x2 = x.reshape(bsz * seq, d).astype(F32)
    for i in range(depth):
        final = i == depth - 1
        w1 = ffn_w1[i].astype(BF16)
        w2 = ffn_w2[i].astype(BF16)
        if i % 2 == 0:
            e = i // 2
            q, k, v, u = _ab_in_proj(x2, norm_mix[i], ab_w_in[e].astype(BF16))
            a = _diff_attention(q, k, v, ab_lambda[e], ab_subln[e], i, bsz, seq)
            y = _pool_concat(a, u, _pool_block_diag(pool_w[e]), pool_scale[e], seq)
            x2 = _mix_ffn(x2, y, ab_w_out[e].astype(BF16), norm_ffn[i], w1, w2, norm_final, final)
        else:
            o = i // 2
            split = 2 * GLA_KEY_DIM + 2 * GLA_VAL_DIM
            w_in = jnp.pad(gla_w_in[o], ((0, 0), (0, GLA_GATE_PAD - GLA_GATE_RANK))).astype(BF16)
            assert w_in.shape[1] == split + GLA_GATE_PAD
            w_up = jnp.pad(gla_w_gk_up[o], ((0, GLA_GATE_PAD - GLA_GATE_RANK), (0, 0))).astype(BF16)
            q, k, v, go, la = _gla_in_proj(x2, norm_mix[i], w_in, w_up, gla_b_gk[o])
            y = _gla_core(q, k, v, go, la, gla_norm[o], bsz, seq)
            x2 = _mix_ffn(x2, y, gla_w_out[o].astype(BF16), norm_ffn[i], w1, w2, norm_final, final)
    return x2.reshape(bsz, seq, d)
```

```python
import functools
import math

import jax
import jax.numpy as jnp
from jax import lax
from jax.experimental import pallas as pl
from jax.experimental.pallas import tpu as pltpu

F32 = jnp.float32
BF16 = jnp.bfloat16

RMS_EPS = 1e-6
CHUNK = 64

DIFF_HEADS = 4
DIFF_HEAD_DIM = 64
DIFF_V_DIM = 2 * DIFF_HEAD_DIM
DIFF_WIDTH = DIFF_HEADS * DIFF_V_DIM
POOL_WINDOWS = (2, 4, 8, 16)
POOL_GROUP_DIM = 128
POOL_WIDTH = POOL_GROUP_DIM * len(POOL_WINDOWS)
POOL_HALO = 16

GLA_HEADS = 4
GLA_HK = 128
GLA_HV = 256
GLA_KEY_DIM = GLA_HEADS * GLA_HK
GLA_VAL_DIM = GLA_HEADS * GLA_HV
GLA_GATE_RANK = 16
GLA_GATE_TEMP = 16.0
GLA_GATE_PAD = 128
GLA_SUB = 16
GLA_EXP_CAP = 60.0

VMEM_LIMIT_BYTES = 56 * 1024 * 1024
TOKEN_TILE = 512
ATTN_TILE = 512
GLA_TILE = 512
ATTN_V_ROWS = DIFF_V_DIM + 16

LOG2E = 1.4426950408889634
NEG_BIG = -1e30


def _dot(a, b):
    return jnp.dot(a, b, preferred_element_type=F32)


def _dot_nt(a, b):
    return lax.dot_general(a, b, (((1,), (1,)), ((), ())), preferred_element_type=F32)


def _dot_tn(a, b):
    return lax.dot_general(a, b, (((0,), (0,)), ((), ())), preferred_element_type=F32)


def _rms(x, gain):
    return x * lax.rsqrt(jnp.mean(x * x, axis=-1, keepdims=True) + RMS_EPS) * gain


def _const_spec(shape):
    nd = len(shape)
    return pl.BlockSpec(shape, lambda *_: (0,) * nd, pipeline_mode=pl.Buffered(1))


def _params(sem):
    return pltpu.CompilerParams(dimension_semantics=sem, vmem_limit_bytes=VMEM_LIMIT_BYTES)


def _ab_in_kernel(x_ref, g_ref, w_ref, q_ref, k_ref, v_ref, u_ref):
    hb = _rms(x_ref[...], g_ref[...]).astype(BF16)
    w = DIFF_WIDTH
    q_ref[...] = (_dot(hb, w_ref[:, 0:w]) * (DIFF_HEAD_DIM ** -0.5 * LOG2E)).astype(BF16)
    k_ref[...] = _dot(hb, w_ref[:, w:2 * w]).astype(BF16)
    v_ref[...] = _dot(hb, w_ref[:, 2 * w:3 * w]).astype(BF16)
    u_ref[...] = _dot(hb, w_ref[:, 3 * w:3 * w + POOL_WIDTH])


def _ab_in_proj(x2, gain, w_in):
    n, d = x2.shape
    tm = TOKEN_TILE
    tok = lambda width: pl.BlockSpec((tm, width), lambda i: (i, 0))
    return pl.pallas_call(
        _ab_in_kernel,
        grid=(n // tm,),
        in_specs=[tok(d), _const_spec((1, d)), _const_spec(w_in.shape)],
        out_specs=[tok(DIFF_WIDTH), tok(DIFF_WIDTH), tok(DIFF_WIDTH), tok(POOL_WIDTH)],
        out_shape=[jax.ShapeDtypeStruct((n, DIFF_WIDTH), BF16)] * 3 + [jax.ShapeDtypeStruct((n, POOL_WIDTH), F32)],
        compiler_params=_params(("parallel",)),
        name="ab_in_proj",
    )(x2, gain.reshape(1, d), w_in)


def _attn_kernel(q_ref, k_ref, vt_ref, bias_ref, lam_ref, gain_ref, o_ref, m_sc, acc_sc, sa_sc, sb_sc, *, tile,
                 lam_init):
    qi = pl.program_id(2)
    q = q_ref[0]
    lane = lax.broadcasted_iota(jnp.int32, q.shape, 1)
    zero = jnp.zeros_like(q)
    qz = jnp.concatenate([jnp.where(lane < DIFF_HEAD_DIM, q, zero), jnp.where(lane >= DIFF_HEAD_DIM, q, zero)], axis=0)

    m_sc[...] = jnp.full(m_sc.shape, NEG_BIG, F32)
    acc_sc[...] = jnp.zeros(acc_sc.shape, F32)

    def scores(i):
        return _dot_nt(k_ref[0, pl.ds(pl.multiple_of(i * tile, tile), tile), :], qz)

    def accumulate(s_ref, i, diagonal):
        bias = bias_ref[...] if diagonal else 0.0
        m_old = m_sc[...]
        m_new = jnp.maximum(m_old, jnp.max(s_ref[...] + bias, axis=0, keepdims=True))
        p = jnp.exp2(s_ref[...] + bias - m_new).astype(BF16)
        acc_sc[...] = acc_sc[...] * jnp.exp2(m_old - m_new) + _dot(vt_ref[0, i], p)
        m_sc[...] = m_new

    sa_sc[...] = scores(0)

    def pair(j, carry):
        i0 = 2 * j
        sb_sc[...] = scores(i0 + 1)
        accumulate(sa_sc, i0, False)
        sa_sc[...] = scores(i0 + 2)
        accumulate(sb_sc, i0 + 1, False)
        return carry

    lax.fori_loop(0, lax.shift_right_logical(qi, 1), pair, 0)
    odd = (qi & 1) == 1

    @pl.when(odd)
    def _():
        sb_sc[...] = scores(qi)
        accumulate(sa_sc, qi - 1, False)
        accumulate(sb_sc, qi, True)

    @pl.when(jnp.logical_not(odd))
    def _():
        accumulate(sa_sc, qi, True)

    acc = acc_sc[...]
    o = acc[:DIFF_V_DIM, :] / acc[DIFF_V_DIM:DIFF_V_DIM + 1, :]
    lp = lam_ref[...]
    lam = (jnp.exp(jnp.sum(lp[0:1] * lp[1:2], axis=1, keepdims=True))
           - jnp.exp(jnp.sum(lp[2:3] * lp[3:4], axis=1, keepdims=True)) + lam_init)
    d = o[:, :tile] - lam * o[:, tile:]
    y = d * lax.rsqrt(jnp.mean(d * d, axis=0, keepdims=True) + RMS_EPS) * gain_ref[...] * (1.0 - lam_init)
    o_ref[0] = y.T.astype(BF16)


def _diff_attention(q, k, v, lam_params, subln_gain, layer_idx, bsz, seq):
    tile = ATTN_TILE
    nblk = seq // tile
    lam_init = 0.8 - 0.6 * math.exp(-0.3 * layer_idx)
    q3 = q.reshape(bsz, seq, DIFF_WIDTH)
    k3 = k.reshape(bsz, seq, DIFF_WIDTH)
    vt = v.reshape(bsz, nblk, tile, DIFF_HEADS, DIFF_V_DIM).transpose(0, 3, 1, 4, 2)
    extra = jnp.zeros((bsz, DIFF_HEADS, nblk, ATTN_V_ROWS - DIFF_V_DIM, tile), BF16).at[:, :, :, 0, :].set(1.0)
    vt = jnp.concatenate([vt, extra], axis=3).reshape(bsz * DIFF_HEADS, nblk, ATTN_V_ROWS, tile)
    gain_b = jnp.broadcast_to(subln_gain.astype(F32).reshape(DIFF_V_DIM, 1), (DIFF_V_DIM, tile))
    chunk_of = jnp.arange(tile) // CHUNK
    bias = jnp.where(chunk_of[:, None] <= chunk_of[None, :], 0.0, NEG_BIG).astype(F32)
    bias = jnp.concatenate([bias, bias], axis=1)
    out = pl.pallas_call(
        functools.partial(_attn_kernel, tile=tile, lam_init=lam_init),
        grid=(bsz, DIFF_HEADS, nblk),
        in_specs=[
            pl.BlockSpec((1, tile, DIFF_V_DIM), lambda b, h, i: (b, i, h)),
            pl.BlockSpec((1, seq, DIFF_V_DIM), lambda b, h, i: (b, 0, h)),
            pl.BlockSpec((1, nblk, ATTN_V_ROWS, tile), lambda b, h, i: (b * DIFF_HEADS + h, 0, 0, 0)),
            _const_spec((tile, 2 * tile)),
            _const_spec((4, DIFF_HEAD_DIM)),
            _const_spec((DIFF_V_DIM, tile)),
        ],
        out_specs=pl.BlockSpec((1, tile, DIFF_V_DIM), lambda b, h, i: (b, i, h)),
        out_shape=jax.ShapeDtypeStruct((bsz, seq, DIFF_WIDTH), BF16),
        scratch_shapes=[pltpu.VMEM((1, 2 * tile), F32), pltpu.VMEM((ATTN_V_ROWS, 2 * tile), F32),
                        pltpu.VMEM((tile, 2 * tile), F32), pltpu.VMEM((tile, 2 * tile), F32)],
        compiler_params=_params(("parallel", "parallel", "arbitrary")),
        name="diff_attention",
    )(q3, k3, vt, bias, lam_params.astype(F32), gain_b)
    return out.reshape(bsz * seq, DIFF_WIDTH)


def _gla_in_kernel(x_ref, g_ref, w_ref, wup_ref, bgk_ref, q_ref, k_ref, v_ref, go_ref, la_ref):
    hb = _rms(x_ref[...], g_ref[...]).astype(BF16)
    kd, vd = GLA_KEY_DIM, GLA_VAL_DIM
    q_ref[...] = (_dot(hb, w_ref[:, 0:kd]) * (GLA_HK ** -0.5)).astype(BF16)
    k_ref[...] = _dot(hb, w_ref[:, kd:2 * kd]).astype(BF16)
    v_ref[...] = _dot(hb, w_ref[:, 2 * kd:2 * kd + vd]).astype(BF16)
    go_ref[...] = _dot(hb, w_ref[:, 2 * kd + vd:2 * kd + 2 * vd]).astype(BF16)
    low = _dot(hb, w_ref[:, 2 * kd + 2 * vd:2 * kd + 2 * vd + GLA_GATE_PAD])
    z = _dot(low.astype(BF16), wup_ref[...]) + bgk_ref[...]
    la_ref[...] = (jnp.minimum(z, 0.0) - jnp.log(1.0 + jnp.exp(-jnp.abs(z)))) * (1.0 / GLA_GATE_TEMP)


def _gla_in_proj(x2, gain, w_in_pad, w_up_pad, b_gk):
    n, d = x2.shape
    tm = TOKEN_TILE
    tok = lambda width: pl.BlockSpec((tm, width), lambda i: (i, 0))
    return pl.pallas_call(
        _gla_in_kernel,
        grid=(n // tm,),
        in_specs=[tok(d), _const_spec((1, d)), _const_spec(w_in_pad.shape), _const_spec(w_up_pad.shape),
                  _const_spec((1, GLA_KEY_DIM))],
        out_specs=[tok(GLA_KEY_DIM), tok(GLA_KEY_DIM), tok(GLA_VAL_DIM), tok(GLA_VAL_DIM), tok(GLA_KEY_DIM)],
        out_shape=[jax.ShapeDtypeStruct((n, GLA_KEY_DIM), BF16), jax.ShapeDtypeStruct((n, GLA_KEY_DIM), BF16),
                   jax.ShapeDtypeStruct((n, GLA_VAL_DIM), BF16), jax.ShapeDtypeStruct((n, GLA_VAL_DIM), BF16),
                   jax.ShapeDtypeStruct((n, GLA_KEY_DIM), F32)],
        compiler_params=_params(("parallel",)),
        name="gla_in_proj",
    )(x2, gain.reshape(1, d), w_in_pad, w_up_pad, b_gk.reshape(1, GLA_KEY_DIM))


def _gla_core_kernel(q_ref, k_ref, v_ref, go_ref, la_ref, gn_ref, o_ref, st_ref, *, tile):
    @pl.when(pl.program_id(2) == 0)
    def _():
        st_ref[...] = jnp.zeros(st_ref.shape, F32)

    rin = lax.broadcasted_iota(jnp.int32, (tile, GLA_HK), 0) & (CHUNK - 1)
    b = la_ref[0]
    sh = 1
    while sh < CHUNK:
        b = b + jnp.where(rin >= sh, pltpu.roll(b, sh, axis=0), 0.0)
        sh *= 2

    q = q_ref[0].astype(F32)
    k = k_ref[0].astype(F32)
    qe = (q * jnp.exp(b)).astype(BF16)
    causal = (lax.broadcasted_iota(jnp.int32, (CHUNK, CHUNK), 0)
              >= lax.broadcasted_iota(jnp.int32, (CHUNK, CHUNK), 1))
    gn = gn_ref[0]
    st = st_ref[...]

    for c in range(tile // CHUNK):
        r0 = c * CHUNK
        bc = b[r0:r0 + CHUNK]
        qc = q[r0:r0 + CHUNK]
        kc = k[r0:r0 + CHUNK]
        b_last = bc[CHUNK - 1:CHUNK]
        kt = (kc * jnp.exp(b_last - bc)).astype(BF16)
        rows = []
        for a in range(CHUNK // GLA_SUB):
            s0 = a * GLA_SUB
            bref = bc[s0:s0 + 1]
            qa = (qc[s0:s0 + GLA_SUB] * jnp.exp(bc[s0:s0 + GLA_SUB] - bref)).astype(BF16)
            ka = (kc * jnp.exp(jnp.minimum(bref - bc, GLA_EXP_CAP))).astype(BF16)
            rows.append(_dot_nt(qa, ka))
        attn = jnp.where(causal, jnp.concatenate(rows, axis=0), 0.0).astype(BF16)
        vc = v_ref[0, r0:r0 + CHUNK, :]
        o = _dot_nt(qe[r0:r0 + CHUNK], st.astype(BF16)) + _dot(attn, vc)
        st = st * jnp.exp(b_last) + _dot_tn(vc, kt)
        g = go_ref[0, r0:r0 + CHUNK, :].astype(F32)
        y = o * lax.rsqrt(jnp.mean(o * o, axis=-1, keepdims=True) + RMS_EPS) * gn
        o_ref[0, r0:r0 + CHUNK, :] = (y * (g / (1.0 + jnp.exp(-g)))).astype(BF16)

    st_ref[...] = st


def _gla_core(q, k, v, go, la, norm_gain, bsz, seq):
    tile = GLA_TILE
    kspec = pl.BlockSpec((1, tile, GLA_HK), lambda b, h, t: (b, t, h))
    vspec = pl.BlockSpec((1, tile, GLA_HV), lambda b, h, t: (b, t, h))
    out = pl.pallas_call(
        functools.partial(_gla_core_kernel, tile=tile),
        grid=(bsz, GLA_HEADS, seq // tile),
        in_specs=[kspec, kspec, vspec, vspec, kspec,
                  pl.BlockSpec((1, 1, GLA_HV), lambda b, h, t: (h, 0, 0))],
        out_specs=vspec,
        out_shape=jax.ShapeDtypeStruct((bsz, seq, GLA_VAL_DIM), BF16),
        scratch_shapes=[pltpu.VMEM((GLA_HV, GLA_HK), F32)],
        compiler_params=_params(("parallel", "parallel", "arbitrary")),
        name="gla_core",
    )(q.reshape(bsz, seq, GLA_KEY_DIM), k.reshape(bsz, seq, GLA_KEY_DIM), v.reshape(bsz, seq, GLA_VAL_DIM),
      go.reshape(bsz, seq, GLA_VAL_DIM), la.reshape(bsz, seq, GLA_KEY_DIM),
      norm_gain.astype(F32).reshape(GLA_HEADS, 1, GLA_HV))
    return out.reshape(bsz * seq, GLA_VAL_DIM)


def _pool_mixer(u, halo, first_tile_of_seq, tile_row0, pw_ref, ps_ref):
    tm = u.shape[0]
    halo = jnp.where(first_tile_of_seq, 0.0, halo)
    ext = jnp.concatenate([halo, u], axis=0)
    pos1 = lax.broadcasted_iota(jnp.int32, (tm, POOL_GROUP_DIM), 0) + (tile_row0 + 1)
    rs = []
    for g, w in enumerate(POOL_WINDOWS):
        cols = slice(g * POOL_GROUP_DIM, (g + 1) * POOL_GROUP_DIM)
        e = ext[:, cols]
        sh = 1
        while sh < w:
            e = e + pltpu.roll(e, sh, axis=0)
            sh *= 2
        cnt = jnp.minimum(pos1, w).astype(F32)
        rs.append((e[POOL_HALO:] / cnt - u[:, cols]).astype(BF16))
    ys = []
    for pair in range(2):
        r2 = jnp.concatenate(rs[2 * pair:2 * pair + 2], axis=1)
        ys.append(_dot(r2, pw_ref[pair]))
    return (jnp.concatenate(ys, axis=1) * ps_ref[...]).astype(BF16)


def _ffn(x, gf_ref, w1_ref, w2_ref):
    hb = _rms(x, gf_ref[...]).astype(BF16)
    d_ff = w1_ref.shape[1]
    step = 1024
    acc = x
    for c0 in range(0, d_ff, step):
        mid = jnp.maximum(_dot(hb, w1_ref[:, c0:c0 + step]), 0.0)
        acc = acc + _dot((mid * mid).astype(BF16), w2_ref[c0:c0 + step, :])
    return acc


def _pool_concat_kernel(a_ref, u_ref, uh_ref, pw_ref, ps_ref, o_ref, *, tiles_per_seq):
    tm = u_ref.shape[0]
    it = pl.program_id(0) % tiles_per_seq
    o_ref[:, 0:DIFF_WIDTH] = a_ref[...]
    o_ref[:, DIFF_WIDTH:] = _pool_mixer(u_ref[...], uh_ref[...], it == 0, it * tm, pw_ref, ps_ref)


def _mix_ffn_kernel(x_ref, y_ref, wo_ref, gf_ref, w1_ref, w2_ref, gl_ref, o_ref, *, final):
    x = x_ref[...] + _dot(y_ref[...], wo_ref[...])
    y = _ffn(x, gf_ref, w1_ref, w2_ref)
    o_ref[...] = _rms(y, gl_ref[...]) if final else y


def _pool_concat(a, u, pw_bd, pool_scale, seq):
    n = a.shape[0]
    tm = TOKEN_TILE
    tok = lambda width: pl.BlockSpec((tm, width), lambda i: (i, 0))
    halo_blocks = tm // POOL_HALO
    return pl.pallas_call(
        functools.partial(_pool_concat_kernel, tiles_per_seq=seq // tm),
        grid=(n // tm,),
        in_specs=[tok(DIFF_WIDTH), tok(POOL_WIDTH),
                  pl.BlockSpec((POOL_HALO, POOL_WIDTH), lambda i: (jnp.maximum(i * halo_blocks - 1, 0), 0)),
                  _const_spec(pw_bd.shape), _const_spec((1, POOL_WIDTH))],
        out_specs=tok(DIFF_WIDTH + POOL_WIDTH),
        out_shape=jax.ShapeDtypeStruct((n, DIFF_WIDTH + POOL_WIDTH), BF16),
        compiler_params=_params(("parallel",)),
        name="pool_concat",
    )(a, u, u, pw_bd, pool_scale.astype(F32).reshape(1, POOL_WIDTH))


def _mix_ffn(x2, y, w_out, gain_ffn, w1, w2, gain_last, final):
    n, d = x2.shape
    tm = TOKEN_TILE
    tok = lambda width: pl.BlockSpec((tm, width), lambda i: (i, 0))
    return pl.pallas_call(
        functools.partial(_mix_ffn_kernel, final=final),
        grid=(n // tm,),
        in_specs=[tok(d), tok(y.shape[1]), _const_spec(w_out.shape), _const_spec((1, d)), _const_spec(w1.shape),
                  _const_spec(w2.shape), _const_spec((1, d))],
        out_specs=tok(d),
        out_shape=jax.ShapeDtypeStruct((n, d), F32),
        compiler_params=_params(("parallel",)),
        name="mix_ffn",
    )(x2, y, w_out, gain_ffn.reshape(1, d), w1, w2, gain_last.reshape(1, d))


def _pool_block_diag(pool_w):
    g = POOL_GROUP_DIM
    z = jnp.zeros((g, g), pool_w.dtype)
    pairs = [jnp.block([[pool_w[2 * p], z], [z, pool_w[2 * p + 1]]]) for p in range(2)]
    return jnp.stack(pairs).astype(BF16)


def kernel(x, norm_mix, norm_ffn, norm_final, ab_w_in, ab_lambda, ab_subln, pool_w, pool_scale, ab_w_out, gla_w_in,
           gla_w_gk_up, gla_b_gk, gla_norm, gla_w_out, ffn_w1, ffn_w2):
    bsz, seq, d = x.shape
    depth = norm_mix.shape[0]
    x2 = x.reshape(bsz * seq, d).astype(F32)
    for i in range(depth):
        final = i == depth - 1
        w1 = ffn_w1[i].astype(BF16)
        w2 = ffn_w2[i].astype(BF16)
        if i % 2 == 0:
            e = i // 2
            q, k, v, u = _ab_in_proj(x2, norm_mix[i], ab_w_in[e].astype(BF16))
            a = _diff_attention(q, k, v, ab_lambda[e], ab_subln[e], i, bsz, seq)
            y = _pool_concat(a, u, _pool_block_diag(pool_w[e]), pool_scale[e], seq)
            x2 = _mix_ffn(x2, y, ab_w_out[e].astype(BF16), norm_ffn[i], w1, w2, norm_final, final)
        else:
            o = i // 2
            split = 2 * GLA_KEY_DIM + 2 * GLA_VAL_DIM
            w_in = jnp.pad(gla_w_in[o], ((0, 0), (0, GLA_GATE_PAD - GLA_GATE_RANK))).astype(BF16)
            assert w_in.shape[1] == split + GLA_GATE_PAD
            w_up = jnp.pad(gla_w_gk_up[o], ((0, GLA_GATE_PAD - GLA_GATE_RANK), (0, 0))).astype(BF16)
            q, k, v, go, la = _gla_in_proj(x2, norm_mix[i], w_in, w_up, gla_b_gk[o])
            y = _gla_core(q, k, v, go, la, gla_norm[o], bsz, seq)
            x2 = _mix_ffn(x2, y, gla_w_out[o].astype(BF16), norm_ffn[i], w1, w2, norm_final, final)
    return x2.reshape(bsz, seq, d)
```

```python
import functools
import math

import jax
import jax.numpy as jnp
from jax import lax
from jax.experimental import pallas as pl
from jax.experimental.pallas import tpu as pltpu

F32 = jnp.float32
BF16 = jnp.bfloat16

RMS_EPS = 1e-6
CHUNK = 64

DIFF_HEADS = 4
DIFF_HEAD_DIM = 64
DIFF_V_DIM = 2 * DIFF_HEAD_DIM
DIFF_WIDTH = DIFF_HEADS * DIFF_V_DIM
POOL_WINDOWS = (2, 4, 8, 16)
POOL_GROUP_DIM = 128
POOL_WIDTH = POOL_GROUP_DIM * len(POOL_WINDOWS)
POOL_HALO = 16

GLA_HEADS = 4
GLA_HK = 128
GLA_HV = 256
GLA_KEY_DIM = GLA_HEADS * GLA_HK
GLA_VAL_DIM = GLA_HEADS * GLA_HV
GLA_GATE_RANK = 16
GLA_GATE_TEMP = 16.0
GLA_GATE_PAD = 128
GLA_SUB = 16
GLA_EXP_CAP = 60.0

VMEM_LIMIT_BYTES = 56 * 1024 * 1024
TOKEN_TILE = 512
ATTN_TILE = 512
GLA_TILE = 512
ATTN_V_ROWS = DIFF_V_DIM + 16

LOG2E = 1.4426950408889634
NEG_BIG = -1e30


def _dot(a, b):
    return jnp.dot(a, b, preferred_element_type=F32)


def _dot_nt(a, b):
    return lax.dot_general(a, b, (((1,), (1,)), ((), ())), preferred_element_type=F32)


def _dot_tn(a, b):
    return lax.dot_general(a, b, (((0,), (0,)), ((), ())), preferred_element_type=F32)


def _rms(x, gain):
    return x * lax.rsqrt(jnp.mean(x * x, axis=-1, keepdims=True) + RMS_EPS) * gain


def _const_spec(shape):
    nd = len(shape)
    return pl.BlockSpec(shape, lambda *_: (0,) * nd, pipeline_mode=pl.Buffered(1))


def _params(sem):
    return pltpu.CompilerParams(dimension_semantics=sem, vmem_limit_bytes=VMEM_LIMIT_BYTES)


def _ab_in_kernel(x_ref, g_ref, w_ref, q_ref, k_ref, v_ref, u_ref):
    hb = _rms(x_ref[...], g_ref[...]).astype(BF16)
    w = DIFF_WIDTH
    q_ref[...] = (_dot(hb, w_ref[:, 0:w]) * (DIFF_HEAD_DIM ** -0.5 * LOG2E)).astype(BF16)
    k_ref[...] = _dot(hb, w_ref[:, w:2 * w]).astype(BF16)
    v_ref[...] = _dot(hb, w_ref[:, 2 * w:3 * w]).astype(BF16)
    u_ref[...] = _dot(hb, w_ref[:, 3 * w:3 * w + POOL_WIDTH])


def _ab_in_proj(x2, gain, w_in):
    n, d = x2.shape
    tm = TOKEN_TILE
    tok = lambda width: pl.BlockSpec((tm, width), lambda i: (i, 0))
    return pl.pallas_call(
        _ab_in_kernel,
        grid=(n // tm,),
        in_specs=[tok(d), _const_spec((1, d)), _const_spec(w_in.shape)],
        out_specs=[tok(DIFF_WIDTH), tok(DIFF_WIDTH), tok(DIFF_WIDTH), tok(POOL_WIDTH)],
        out_shape=[jax.ShapeDtypeStruct((n, DIFF_WIDTH), BF16)] * 3 + [jax.ShapeDtypeStruct((n, POOL_WIDTH), F32)],
        compiler_params=_params(("parallel",)),
        name="ab_in_proj",
    )(x2, gain.reshape(1, d), w_in)


def _attn_kernel(q_ref, k_ref, vt_ref, bias_ref, lam_ref, gain_ref, o_ref, m_sc, acc_sc, sa_sc, sb_sc, ma_sc, mb_sc, *,
                 tile, lam_init):
    qi = pl.program_id(2)
    q = q_ref[0]
    lane = lax.broadcasted_iota(jnp.int32, q.shape, 1)
    zero = jnp.zeros_like(q)
    qz = jnp.concatenate([jnp.where(lane < DIFF_HEAD_DIM, q, zero), jnp.where(lane >= DIFF_HEAD_DIM, q, zero)], axis=0)

    m_sc[...] = jnp.full(m_sc.shape, NEG_BIG, F32)
    acc_sc[...] = jnp.zeros(acc_sc.shape, F32)

    def produce(i, s_ref, mx_ref, diagonal):
        s = _dot_nt(k_ref[0, pl.ds(pl.multiple_of(i * tile, tile), tile), :], qz)
        if diagonal:
            s = s + bias_ref[...]
        s_ref[...] = s
        mx_ref[...] = jnp.max(s, axis=0, keepdims=True)

    def consume(s_ref, mx_ref, i):
        m_old = m_sc[...]
        m_new = jnp.maximum(m_old, mx_ref[...])
        p = jnp.exp2(s_ref[...] - m_new).astype(BF16)
        acc_sc[...] = acc_sc[...] * jnp.exp2(m_old - m_new) + _dot(vt_ref[0, i], p)
        m_sc[...] = m_new

    @pl.when(qi > 0)
    def _():
        produce(0, sa_sc, ma_sc, False)

    def pair(j, carry):
        i0 = 2 * j
        produce(i0 + 1, sb_sc, mb_sc, False)
        consume(sa_sc, ma_sc, i0)
        produce(i0 + 2, sa_sc, ma_sc, False)
        consume(sb_sc, mb_sc, i0 + 1)
        return carry

    lax.fori_loop(0, jnp.maximum(lax.shift_right_arithmetic(qi - 1, 1), 0), pair, 0)
    odd = (qi & 1) == 1

    @pl.when(qi == 0)
    def _():
        produce(0, sa_sc, ma_sc, True)
        consume(sa_sc, ma_sc, 0)

    @pl.when(odd)
    def _():
        produce(qi, sb_sc, mb_sc, True)
        consume(sa_sc, ma_sc, qi - 1)
        consume(sb_sc, mb_sc, qi)

    @pl.when(jnp.logical_and(jnp.logical_not(odd), qi > 0))
    def _():
        produce(qi - 1, sb_sc, mb_sc, False)
        consume(sa_sc, ma_sc, qi - 2)
        produce(qi, sa_sc, ma_sc, True)
        consume(sb_sc, mb_sc, qi - 1)
        consume(sa_sc, ma_sc, qi)

    acc = acc_sc[...]
    o = acc[:DIFF_V_DIM, :] / acc[DIFF_V_DIM:DIFF_V_DIM + 1, :]
    lp = lam_ref[...]
    lam = (jnp.exp(jnp.sum(lp[0:1] * lp[1:2], axis=1, keepdims=True))
           - jnp.exp(jnp.sum(lp[2:3] * lp[3:4], axis=1, keepdims=True)) + lam_init)
    d = o[:, :tile] - lam * o[:, tile:]
    y = d * lax.rsqrt(jnp.mean(d * d, axis=0, keepdims=True) + RMS_EPS) * gain_ref[...] * (1.0 - lam_init)
    o_ref[0] = y.T.astype(BF16)


def _diff_attention(q, k, v, lam_params, subln_gain, layer_idx, bsz, seq):
    tile = ATTN_TILE
    nblk = seq // tile
    lam_init = 0.8 - 0.6 * math.exp(-0.3 * layer_idx)
    q3 = q.reshape(bsz, seq, DIFF_WIDTH)
    k3 = k.reshape(bsz, seq, DIFF_WIDTH)
    vt = v.reshape(bsz, nblk, tile, DIFF_HEADS, DIFF_V_DIM).transpose(0, 3, 1, 4, 2)
    extra = jnp.zeros((bsz, DIFF_HEADS, nblk, ATTN_V_ROWS - DIFF_V_DIM, tile), BF16).at[:, :, :, 0, :].set(1.0)
    vt = jnp.concatenate([vt, extra], axis=3).reshape(bsz * DIFF_HEADS, nblk, ATTN_V_ROWS, tile)
    gain_b = jnp.broadcast_to(subln_gain.astype(F32).reshape(DIFF_V_DIM, 1), (DIFF_V_DIM, tile))
    chunk_of = jnp.arange(tile) // CHUNK
    bias = jnp.where(chunk_of[:, None] <= chunk_of[None, :], 0.0, NEG_BIG).astype(F32)
    bias = jnp.concatenate([bias, bias], axis=1)
    out = pl.pallas_call(
        functools.partial(_attn_kernel, tile=tile, lam_init=lam_init),
        grid=(bsz, DIFF_HEADS, nblk),
        in_specs=[
            pl.BlockSpec((1, tile, DIFF_V_DIM), lambda b, h, i: (b, i, h)),
            pl.BlockSpec((1, seq, DIFF_V_DIM), lambda b, h, i: (b, 0, h)),
            pl.BlockSpec((1, nblk, ATTN_V_ROWS, tile), lambda b, h, i: (b * DIFF_HEADS + h, 0, 0, 0)),
            _const_spec((tile, 2 * tile)),
            _const_spec((4, DIFF_HEAD_DIM)),
            _const_spec((DIFF_V_DIM, tile)),
        ],
        out_specs=pl.BlockSpec((1, tile, DIFF_V_DIM), lambda b, h, i: (b, i, h)),
        out_shape=jax.ShapeDtypeStruct((bsz, seq, DIFF_WIDTH), BF16),
        scratch_shapes=[pltpu.VMEM((1, 2 * tile), F32), pltpu.VMEM((ATTN_V_ROWS, 2 * tile), F32),
                        pltpu.VMEM((tile, 2 * tile), F32), pltpu.VMEM((tile, 2 * tile), F32),
                        pltpu.VMEM((1, 2 * tile), F32), pltpu.VMEM((1, 2 * tile), F32)],
        compiler_params=_params(("parallel", "parallel", "arbitrary")),
        name="diff_attention",
    )(q3, k3, vt, bias, lam_params.astype(F32), gain_b)
    return out.reshape(bsz * seq, DIFF_WIDTH)


def _gla_in_kernel(x_ref, g_ref, w_ref, wup_ref, bgk_ref, q_ref, k_ref, v_ref, go_ref, la_ref):
    hb = _rms(x_ref[...], g_ref[...]).astype(BF16)
    kd, vd = GLA_KEY_DIM, GLA_VAL_DIM
    q_ref[...] = (_dot(hb, w_ref[:, 0:kd]) * (GLA_HK ** -0.5)).astype(BF16)
    k_ref[...] = _dot(hb, w_ref[:, kd:2 * kd]).astype(BF16)
    v_ref[...] = _dot(hb, w_ref[:, 2 * kd:2 * kd + vd]).astype(BF16)
    go_ref[...] = _dot(hb, w_ref[:, 2 * kd + vd:2 * kd + 2 * vd]).astype(BF16)
    low = _dot(hb, w_ref[:, 2 * kd + 2 * vd:2 * kd + 2 * vd + GLA_GATE_PAD])
    z = _dot(low.astype(BF16), wup_ref[...]) + bgk_ref[...]
    la_ref[...] = (jnp.minimum(z, 0.0) - jnp.log(1.0 + jnp.exp(-jnp.abs(z)))) * (1.0 / GLA_GATE_TEMP)


def _gla_in_proj(x2, gain, w_in_pad, w_up_pad, b_gk):
    n, d = x2.shape
    tm = TOKEN_TILE
    tok = lambda width: pl.BlockSpec((tm, width), lambda i: (i, 0))
    return pl.pallas_call(
        _gla_in_kernel,
        grid=(n // tm,),
        in_specs=[tok(d), _const_spec((1, d)), _const_spec(w_in_pad.shape), _const_spec(w_up_pad.shape),
                  _const_spec((1, GLA_KEY_DIM))],
        out_specs=[tok(GLA_KEY_DIM), tok(GLA_KEY_DIM), tok(GLA_VAL_DIM), tok(GLA_VAL_DIM), tok(GLA_KEY_DIM)],
        out_shape=[jax.ShapeDtypeStruct((n, GLA_KEY_DIM), BF16), jax.ShapeDtypeStruct((n, GLA_KEY_DIM), BF16),
                   jax.ShapeDtypeStruct((n, GLA_VAL_DIM), BF16), jax.ShapeDtypeStruct((n, GLA_VAL_DIM), BF16),
                   jax.ShapeDtypeStruct((n, GLA_KEY_DIM), F32)],
        compiler_params=_params(("parallel",)),
        name="gla_in_proj",
    )(x2, gain.reshape(1, d), w_in_pad, w_up_pad, b_gk.reshape(1, GLA_KEY_DIM))


def _gla_core_kernel(q_ref, k_ref, v_ref, go_ref, la_ref, gn_ref, o_ref, st_ref, *, tile):
    @pl.when(pl.program_id(2) == 0)
    def _():
        st_ref[...] = jnp.zeros(st_ref.shape, F32)

    rin = lax.broadcasted_iota(jnp.int32, (tile, GLA_HK), 0) & (CHUNK - 1)
    b = la_ref[0]
    sh = 1
    while sh < CHUNK:
        b = b + jnp.where(rin >= sh, pltpu.roll(b, sh, axis=0), 0.0)
        sh *= 2

    q = q_ref[0].astype(F32)
    k = k_ref[0].astype(F32)
    qe = (q * jnp.exp(b)).astype(BF16)
    causal = (lax.broadcasted_iota(jnp.int32, (CHUNK, CHUNK), 0)
              >= lax.broadcasted_iota(jnp.int32, (CHUNK, CHUNK), 1))
    gn = gn_ref[0]
    st = st_ref[...]

    for c in range(tile // CHUNK):
        r0 = c * CHUNK
        bc = b[r0:r0 + CHUNK]
        qc = q[r0:r0 + CHUNK]
        kc = k[r0:r0 + CHUNK]
        b_last = bc[CHUNK - 1:CHUNK]
        kt = (kc * jnp.exp(b_last - bc)).astype(BF16)
        rows = []
        for a in range(CHUNK // GLA_SUB):
            s0 = a * GLA_SUB
            bref = bc[s0:s0 + 1]
            qa = (qc[s0:s0 + GLA_SUB] * jnp.exp(bc[s0:s0 + GLA_SUB] - bref)).astype(BF16)
            ka = (kc * jnp.exp(jnp.minimum(bref - bc, GLA_EXP_CAP))).astype(BF16)
            rows.append(_dot_nt(qa, ka))
        attn = jnp.where(causal, jnp.concatenate(rows, axis=0), 0.0).astype(BF16)
        vc = v_ref[0, r0:r0 + CHUNK, :]
        o = _dot_nt(qe[r0:r0 + CHUNK], st.astype(BF16)) + _dot(attn, vc)
        st = st * jnp.exp(b_last) + _dot_tn(vc, kt)
        g = go_ref[0, r0:r0 + CHUNK, :].astype(F32)
        y = o * lax.rsqrt(jnp.mean(o * o, axis=-1, keepdims=True) + RMS_EPS) * gn
        o_ref[0, r0:r0 + CHUNK, :] = (y * (g / (1.0 + jnp.exp(-g)))).astype(BF16)

    st_ref[...] = st


def _gla_core(q, k, v, go, la, norm_gain, bsz, seq):
    tile = GLA_TILE
    kspec = pl.BlockSpec((1, tile, GLA_HK), lambda b, h, t: (b, t, h))
    vspec = pl.BlockSpec((1, tile, GLA_HV), lambda b, h, t: (b, t, h))
    out = pl.pallas_call(
        functools.partial(_gla_core_kernel, tile=tile),
        grid=(bsz, GLA_HEADS, seq // tile),
        in_specs=[kspec, kspec, vspec, vspec, kspec,
                  pl.BlockSpec((1, 1, GLA_HV), lambda b, h, t: (h, 0, 0))],
        out_specs=vspec,
        out_shape=jax.ShapeDtypeStruct((bsz, seq, GLA_VAL_DIM), BF16),
        scratch_shapes=[pltpu.VMEM((GLA_HV, GLA_HK), F32)],
        compiler_params=_params(("parallel", "parallel", "arbitrary")),
        name="gla_core",
    )(q.reshape(bsz, seq, GLA_KEY_DIM), k.reshape(bsz, seq, GLA_KEY_DIM), v.reshape(bsz, seq, GLA_VAL_DIM),
      go.reshape(bsz, seq, GLA_VAL_DIM), la.reshape(bsz, seq, GLA_KEY_DIM),
      norm_gain.astype(F32).reshape(GLA_HEADS, 1, GLA_HV))
    return out.reshape(bsz * seq, GLA_VAL_DIM)


def _pool_mixer(u, halo, first_tile_of_seq, tile_row0, pw_ref, ps_ref):
    tm = u.shape[0]
    halo = jnp.where(first_tile_of_seq, 0.0, halo)
    ext = jnp.concatenate([halo, u], axis=0)
    pos1 = lax.broadcasted_iota(jnp.int32, (tm, POOL_GROUP_DIM), 0) + (tile_row0 + 1)
    rs = []
    for g, w in enumerate(POOL_WINDOWS):
        cols = slice(g * POOL_GROUP_DIM, (g + 1) * POOL_GROUP_DIM)
        e = ext[:, cols]
        sh = 1
        while sh < w:
            e = e + pltpu.roll(e, sh, axis=0)
            sh *= 2
        cnt = jnp.minimum(pos1, w).astype(F32)
        rs.append((e[POOL_HALO:] / cnt - u[:, cols]).astype(BF16))
    ys = []
    for pair in range(2):
        r2 = jnp.concatenate(rs[2 * pair:2 * pair + 2], axis=1)
        ys.append(_dot(r2, pw_ref[pair]))
    return (jnp.concatenate(ys, axis=1) * ps_ref[...]).astype(BF16)


def _ffn(x, gf_ref, w1_ref, w2_ref):
    hb = _rms(x, gf_ref[...]).astype(BF16)
    d_ff = w1_ref.shape[1]
    step = 1024
    acc = x
    for c0 in range(0, d_ff, step):
        mid = jnp.maximum(_dot(hb, w1_ref[:, c0:c0 + step]), 0.0)
        acc = acc + _dot((mid * mid).astype(BF16), w2_ref[c0:c0 + step, :])
    return acc


def _pool_concat_kernel(a_ref, u_ref, uh_ref, pw_ref, ps_ref, o_ref, *, tiles_per_seq):
    tm = u_ref.shape[0]
    it = pl.program_id(0) % tiles_per_seq
    o_ref[:, 0:DIFF_WIDTH] = a_ref[...]
    o_ref[:, DIFF_WIDTH:] = _pool_mixer(u_ref[...], uh_ref[...], it == 0, it * tm, pw_ref, ps_ref)


def _mix_ffn_kernel(x_ref, y_ref, wo_ref, gf_ref, w1_ref, w2_ref, gl_ref, o_ref, *, final):
    x = x_ref[...] + _dot(y_ref[...], wo_ref[...])
    y = _ffn(x, gf_ref, w1_ref, w2_ref)
    o_ref[...] = _rms(y, gl_ref[...]) if final else y


def _pool_concat(a, u, pw_bd, pool_scale, seq):
    n = a.shape[0]
    tm = TOKEN_TILE
    tok = lambda width: pl.BlockSpec((tm, width), lambda i: (i, 0))
    halo_blocks = tm // POOL_HALO
    return pl.pallas_call(
        functools.partial(_pool_concat_kernel, tiles_per_seq=seq // tm),
        grid=(n // tm,),
        in_specs=[tok(DIFF_WIDTH), tok(POOL_WIDTH),
                  pl.BlockSpec((POOL_HALO, POOL_WIDTH), lambda i: (jnp.maximum(i * halo_blocks - 1, 0), 0)),
                  _const_spec(pw_bd.shape), _const_spec((1, POOL_WIDTH))],
        out_specs=tok(DIFF_WIDTH + POOL_WIDTH),
        out_shape=jax.ShapeDtypeStruct((n, DIFF_WIDTH + POOL_WIDTH), BF16),
        compiler_params=_params(("parallel",)),
        name="pool_concat",
    )(a, u, u, pw_bd, pool_scale.astype(F32).reshape(1, POOL_WIDTH))


def _mix_ffn(x2, y, w_out, gain_ffn, w1, w2, gain_last, final):
    n, d = x2.shape
    tm = TOKEN_TILE
    tok = lambda width: pl.BlockSpec((tm, width), lambda i: (i, 0))
    return pl.pallas_call(
        functools.partial(_mix_ffn_kernel, final=final),
        grid=(n // tm,),
        in_specs=[tok(d), tok(y.shape[1]), _const_spec(w_out.shape), _const_spec((1, d)), _const_spec(w1.shape),
                  _const_spec(w2.shape), _const_spec((1, d))],
        out_specs=tok(d),
        out_shape=jax.ShapeDtypeStruct((n, d), F32),
        compiler_params=_params(("parallel",)),
        name="mix_ffn",
    )(x2, y, w_out, gain_ffn.reshape(1, d), w1, w2, gain_last.reshape(1, d))


def _pool_block_diag(pool_w):
    g = POOL_GROUP_DIM
    z = jnp.zeros((g, g), pool_w.dtype)
    pairs = [jnp.block([[pool_w[2 * p], z], [z, pool_w[2 * p + 1]]]) for p in range(2)]
    return jnp.stack(pairs).astype(BF16)


def kernel(x, norm_mix, norm_ffn, norm_final, ab_w_in, ab_lambda, ab_subln, pool_w, pool_scale, ab_w_out, gla_w_in,
           gla_w_gk_up, gla_b_gk, gla_norm, gla_w_out, ffn_w1, ffn_w2):
    bsz, seq, d = x.shape
    depth = norm_mix.shape[0]
    x2 = x.reshape(bsz * seq, d).astype(F32)
    for i in range(depth):
        final = i == depth - 1
        w1 = ffn_w1[i].astype(BF16)
        w2 = ffn_w2[i].astype(BF16)
        if i % 2 == 0:
            e = i // 2
            q, k, v, u = _ab_in_proj(x2, norm_mix[i], ab_w_in[e].astype(BF16))
            a = _diff_attention(q, k, v, ab_lambda[e], ab_subln[e], i, bsz, seq)
            y = _pool_concat(a, u, _pool_block_diag(pool_w[e]), pool_scale[e], seq)
            x2 = _mix_ffn(x2, y, ab_w_out[e].astype(BF16), norm_ffn[i], w1, w2, norm_final, final)
        else:
            o = i // 2
            split = 2 * GLA_KEY_DIM + 2 * GLA_VAL_DIM
            w_in = jnp.pad(gla_w_in[o], ((0, 0), (0, GLA_GATE_PAD - GLA_GATE_RANK))).astype(BF16)
            assert w_in.shape[1] == split + GLA_GATE_PAD
            w_up = jnp.pad(gla_w_gk_up[o], ((0, GLA_GATE_PAD - GLA_GATE_RANK), (0, 0))).astype(BF16)
            q, k, v, go, la = _gla_in_proj(x2, norm_mix[i], w_in, w_up, gla_b_gk[o])
            y = _gla_core(q, k, v, go, la, gla_norm[o], bsz, seq)
            x2 = _mix_ffn(x2, y, gla_w_out[o].astype(BF16), norm_ffn[i], w1, w2, norm_final, final)
    return x2.reshape(bsz, seq, d)
```

```python
import functools
import math

import jax
import jax.numpy as jnp
from jax import lax
from jax.experimental import pallas as pl
from jax.experimental.pallas import tpu as pltpu

F32 = jnp.float32
BF16 = jnp.bfloat16

RMS_EPS = 1e-6
CHUNK = 64

DIFF_HEADS = 4
DIFF_HEAD_DIM = 64
DIFF_V_DIM = 2 * DIFF_HEAD_DIM
DIFF_WIDTH = DIFF_HEADS * DIFF_V_DIM
POOL_WINDOWS = (2, 4, 8, 16)
POOL_GROUP_DIM = 128
POOL_WIDTH = POOL_GROUP_DIM * len(POOL_WINDOWS)
POOL_HALO = 16

GLA_HEADS = 4
GLA_HK = 128
GLA_HV = 256
GLA_KEY_DIM = GLA_HEADS * GLA_HK
GLA_VAL_DIM = GLA_HEADS * GLA_HV
GLA_GATE_RANK = 16
GLA_GATE_TEMP = 16.0
GLA_GATE_PAD = 128
GLA_SUB = 16
GLA_EXP_CAP = 60.0

VMEM_LIMIT_BYTES = 56 * 1024 * 1024
TOKEN_TILE = 512
GLA_IN_TILE = 1024
ATTN_Q_TILE = 1024
ATTN_K_TILE = 512
GLA_TILE = 1024
ATTN_V_ROWS = DIFF_V_DIM + 16

LOG2E = 1.4426950408889634
NEG_BIG = -1e30


def _dot(a, b):
    return jnp.dot(a, b, preferred_element_type=F32)


def _dot_nt(a, b):
    return lax.dot_general(a, b, (((1,), (1,)), ((), ())), preferred_element_type=F32)


def _dot_tn(a, b):
    return lax.dot_general(a, b, (((0,), (0,)), ((), ())), preferred_element_type=F32)


def _rms(x, gain):
    return x * lax.rsqrt(jnp.mean(x * x, axis=-1, keepdims=True) + RMS_EPS) * gain


def _const_spec(shape):
    nd = len(shape)
    return pl.BlockSpec(shape, lambda *_: (0,) * nd, pipeline_mode=pl.Buffered(1))


def _params(sem):
    return pltpu.CompilerParams(dimension_semantics=sem, vmem_limit_bytes=VMEM_LIMIT_BYTES)


def _ab_in_kernel(x_ref, g_ref, w_ref, q_ref, k_ref, vt_ref, u_ref):
    hb = _rms(x_ref[...], g_ref[...]).astype(BF16)
    w = DIFF_WIDTH
    q_ref[...] = (_dot(hb, w_ref[:, 0:w]) * (DIFF_HEAD_DIM ** -0.5 * LOG2E)).astype(BF16)
    k_ref[...] = _dot(hb, w_ref[:, w:2 * w]).astype(BF16)
    u_ref[...] = _dot(hb, w_ref[:, 3 * w:3 * w + POOL_WIDTH])
    v = _dot(hb, w_ref[:, 2 * w:3 * w])
    tm = v.shape[0]
    pad_rows = lax.broadcasted_iota(jnp.int32, (ATTN_V_ROWS - DIFF_V_DIM, tm), 0)
    pad = jnp.where(pad_rows == 0, 1.0, 0.0).astype(BF16)
    for h in range(DIFF_HEADS):
        vt_ref[0, h, 0, 0:DIFF_V_DIM, :] = v[:, h * DIFF_V_DIM:(h + 1) * DIFF_V_DIM].T.astype(BF16)
        vt_ref[0, h, 0, DIFF_V_DIM:, :] = pad


def _ab_in_proj(x2, gain, w_in, bsz, seq):
    n, d = x2.shape
    tm = ATTN_K_TILE
    nkb = seq // tm
    tok = lambda width: pl.BlockSpec((tm, width), lambda i: (i, 0))
    return pl.pallas_call(
        _ab_in_kernel,
        grid=(n // tm,),
        in_specs=[tok(d), _const_spec((1, d)), _const_spec(w_in.shape)],
        out_specs=[tok(DIFF_WIDTH), tok(DIFF_WIDTH),
                   pl.BlockSpec((1, DIFF_HEADS, 1, ATTN_V_ROWS, tm), lambda i: (i // nkb, 0, i % nkb, 0, 0)),
                   tok(POOL_WIDTH)],
        out_shape=[jax.ShapeDtypeStruct((n, DIFF_WIDTH), BF16), jax.ShapeDtypeStruct((n, DIFF_WIDTH), BF16),
                   jax.ShapeDtypeStruct((bsz, DIFF_HEADS, nkb, ATTN_V_ROWS, tm), BF16),
                   jax.ShapeDtypeStruct((n, POOL_WIDTH), F32)],
        compiler_params=_params(("parallel",)),
        name="ab_in_proj",
    )(x2, gain.reshape(1, d), w_in)


def _attn_kernel(q_ref, k_ref, vt_ref, bias_ref, lam_ref, gain_ref, o_ref, m_sc, acc_sc, sa_sc, sb_sc, ma_sc, mb_sc, *,
                 tq, tk, lam_init):
    qi = pl.program_id(2)
    q = q_ref[0]
    lane = lax.broadcasted_iota(jnp.int32, q.shape, 1)
    zero = jnp.zeros_like(q)
    qz = jnp.concatenate([jnp.where(lane < DIFF_HEAD_DIM, q, zero), jnp.where(lane >= DIFF_HEAD_DIM, q, zero)], axis=0)

    m_sc[...] = jnp.full(m_sc.shape, NEG_BIG, F32)
    acc_sc[...] = jnp.zeros(acc_sc.shape, F32)

    def produce(i, s_ref, mx_ref, diagonal):
        s = _dot_nt(k_ref[0, pl.ds(pl.multiple_of(i * tk, tk), tk), :], qz)
        if diagonal is not None:
            s = s + bias_ref[diagonal]
        s_ref[...] = s
        mx_ref[...] = jnp.max(s, axis=0, keepdims=True)

    def consume(s_ref, mx_ref, i):
        m_old = m_sc[...]
        m_new = jnp.maximum(m_old, mx_ref[...])
        p = jnp.exp2(s_ref[...] - m_new).astype(BF16)
        acc_sc[...] = acc_sc[...] * jnp.exp2(m_old - m_new) + _dot(vt_ref[0, i], p)
        m_sc[...] = m_new

    @pl.when(qi > 0)
    def _():
        produce(0, sa_sc, ma_sc, None)

    @pl.when(qi == 0)
    def _():
        produce(0, sa_sc, ma_sc, 0)

    def pair(j, carry):
        i0 = 2 * j
        produce(i0 + 1, sb_sc, mb_sc, None)
        consume(sa_sc, ma_sc, i0)
        produce(i0 + 2, sa_sc, ma_sc, None)
        consume(sb_sc, mb_sc, i0 + 1)
        return carry

    lax.fori_loop(0, jnp.maximum(qi - 1, 0), pair, 0)

    @pl.when(qi > 0)
    def _():
        produce(2 * qi - 1, sb_sc, mb_sc, None)
        consume(sa_sc, ma_sc, 2 * qi - 2)
        produce(2 * qi, sa_sc, ma_sc, 0)
        consume(sb_sc, mb_sc, 2 * qi - 1)

    produce(2 * qi + 1, sb_sc, mb_sc, 1)
    consume(sa_sc, ma_sc, 2 * qi)
    consume(sb_sc, mb_sc, 2 * qi + 1)

    acc = acc_sc[...]
    o = acc[:DIFF_V_DIM, :] / acc[DIFF_V_DIM:DIFF_V_DIM + 1, :]
    lp = lam_ref[...]
    lam = (jnp.exp(jnp.sum(lp[0:1] * lp[1:2], axis=1, keepdims=True))
           - jnp.exp(jnp.sum(lp[2:3] * lp[3:4], axis=1, keepdims=True)) + lam_init)
    d = o[:, :tq] - lam * o[:, tq:]
    y = d * lax.rsqrt(jnp.mean(d * d, axis=0, keepdims=True) + RMS_EPS) * gain_ref[...] * (1.0 - lam_init)
    o_ref[0] = y.T.astype(BF16)


def _diff_attention(q, k, vt, lam_params, subln_gain, layer_idx, bsz, seq):
    tq, tk = ATTN_Q_TILE, ATTN_K_TILE
    assert tq == 2 * tk
    nkb = seq // tk
    lam_init = 0.8 - 0.6 * math.exp(-0.3 * layer_idx)
    q3 = q.reshape(bsz, seq, DIFF_WIDTH)
    k3 = k.reshape(bsz, seq, DIFF_WIDTH)
    vt = vt.reshape(bsz * DIFF_HEADS, nkb, ATTN_V_ROWS, tk)
    gain_b = jnp.broadcast_to(subln_gain.astype(F32).reshape(DIFF_V_DIM, 1), (DIFF_V_DIM, tq))
    key_chunk = jnp.arange(tq).reshape(2, tk) // CHUNK
    qry_chunk = jnp.arange(tq) // CHUNK
    bias = jnp.where(key_chunk[:, :, None] <= qry_chunk[None, None, :], 0.0, NEG_BIG).astype(F32)
    bias = jnp.concatenate([bias, bias], axis=2)
    out = pl.pallas_call(
        functools.partial(_attn_kernel, tq=tq, tk=tk, lam_init=lam_init),
        grid=(bsz, DIFF_HEADS, seq // tq),
        in_specs=[
            pl.BlockSpec((1, tq, DIFF_V_DIM), lambda b, h, i: (b, i, h)),
            pl.BlockSpec((1, seq, DIFF_V_DIM), lambda b, h, i: (b, 0, h)),
            pl.BlockSpec((1, nkb, ATTN_V_ROWS, tk), lambda b, h, i: (b * DIFF_HEADS + h, 0, 0, 0)),
            _const_spec((2, tk, 2 * tq)),
            _const_spec((4, DIFF_HEAD_DIM)),
            _const_spec((DIFF_V_DIM, tq)),
        ],
        out_specs=pl.BlockSpec((1, tq, DIFF_V_DIM), lambda b, h, i: (b, i, h)),
        out_shape=jax.ShapeDtypeStruct((bsz, seq, DIFF_WIDTH), BF16),
        scratch_shapes=[pltpu.VMEM((1, 2 * tq), F32), pltpu.VMEM((ATTN_V_ROWS, 2 * tq), F32),
                        pltpu.VMEM((tk, 2 * tq), F32), pltpu.VMEM((tk, 2 * tq), F32),
                        pltpu.VMEM((1, 2 * tq), F32), pltpu.VMEM((1, 2 * tq), F32)],
        compiler_params=_params(("parallel", "parallel", "arbitrary")),
        name="diff_attention",
    )(q3, k3, vt, bias, lam_params.astype(F32), gain_b)
    return out.reshape(bsz * seq, DIFF_WIDTH)


def _gla_in_kernel(x_ref, g_ref, w_ref, wup_ref, bgk_ref, q_ref, k_ref, v_ref, go_ref, la_ref):
    hb = _rms(x_ref[...], g_ref[...]).astype(BF16)
    kd, vd = GLA_KEY_DIM, GLA_VAL_DIM
    q_ref[...] = (_dot(hb, w_ref[:, 0:kd]) * (GLA_HK ** -0.5)).astype(BF16)
    k_ref[...] = _dot(hb, w_ref[:, kd:2 * kd]).astype(BF16)
    v_ref[...] = _dot(hb, w_ref[:, 2 * kd:2 * kd + vd]).astype(BF16)
    go_ref[...] = _dot(hb, w_ref[:, 2 * kd + vd:2 * kd + 2 * vd]).astype(BF16)
    low = _dot(hb, w_ref[:, 2 * kd + 2 * vd:2 * kd + 2 * vd + GLA_GATE_PAD])
    z = _dot(low.astype(BF16), wup_ref[...]) + bgk_ref[...]
    la_ref[...] = (jnp.minimum(z, 0.0) - jnp.log(1.0 + jnp.exp(-jnp.abs(z)))) * (1.0 / GLA_GATE_TEMP)


def _gla_in_proj(x2, gain, w_in_pad, w_up_pad, b_gk):
    n, d = x2.shape
    tm = GLA_IN_TILE
    tok = lambda width: pl.BlockSpec((tm, width), lambda i: (i, 0))
    return pl.pallas_call(
        _gla_in_kernel,
        grid=(n // tm,),
        in_specs=[tok(d), _const_spec((1, d)), _const_spec(w_in_pad.shape), _const_spec(w_up_pad.shape),
                  _const_spec((1, GLA_KEY_DIM))],
        out_specs=[tok(GLA_KEY_DIM), tok(GLA_KEY_DIM), tok(GLA_VAL_DIM), tok(GLA_VAL_DIM), tok(GLA_KEY_DIM)],
        out_shape=[jax.ShapeDtypeStruct((n, GLA_KEY_DIM), BF16), jax.ShapeDtypeStruct((n, GLA_KEY_DIM), BF16),
                   jax.ShapeDtypeStruct((n, GLA_VAL_DIM), BF16), jax.ShapeDtypeStruct((n, GLA_VAL_DIM), BF16),
                   jax.ShapeDtypeStruct((n, GLA_KEY_DIM), F32)],
        compiler_params=_params(("parallel",)),
        name="gla_in_proj",
    )(x2, gain.reshape(1, d), w_in_pad, w_up_pad, b_gk.reshape(1, GLA_KEY_DIM))


def _gla_core_kernel(q_ref, k_ref, v_ref, go_ref, la_ref, gn_ref, o_ref, st_ref, *, tile):
    @pl.when(pl.program_id(2) == 0)
    def _():
        st_ref[...] = jnp.zeros(st_ref.shape, F32)

    rin = lax.broadcasted_iota(jnp.int32, (tile, GLA_HK), 0) & (CHUNK - 1)
    b = la_ref[0]
    sh = 1
    while sh < CHUNK:
        b = b + jnp.where(rin >= sh, pltpu.roll(b, sh, axis=0), 0.0)
        sh *= 2

    q = q_ref[0].astype(F32)
    k = k_ref[0].astype(F32)
    qe = (q * jnp.exp(b)).astype(BF16)
    causal = (lax.broadcasted_iota(jnp.int32, (CHUNK, CHUNK), 0)
              >= lax.broadcasted_iota(jnp.int32, (CHUNK, CHUNK), 1))
    gn = gn_ref[0]
    st = st_ref[...]

    for c in range(tile // CHUNK):
        r0 = c * CHUNK
        bc = b[r0:r0 + CHUNK]
        qc = q[r0:r0 + CHUNK]
        kc = k[r0:r0 + CHUNK]
        b_last = bc[CHUNK - 1:CHUNK]
        kt = (kc * jnp.exp(b_last - bc)).astype(BF16)
        rows = []
        for a in range(CHUNK // GLA_SUB):
            s0 = a * GLA_SUB
            bref = bc[s0:s0 + 1]
            qa = (qc[s0:s0 + GLA_SUB] * jnp.exp(bc[s0:s0 + GLA_SUB] - bref)).astype(BF16)
            ka = (kc * jnp.exp(jnp.minimum(bref - bc, GLA_EXP_CAP))).astype(BF16)
            rows.append(_dot_nt(qa, ka))
        attn = jnp.where(causal, jnp.concatenate(rows, axis=0), 0.0).astype(BF16)
        vc = v_ref[0, r0:r0 + CHUNK, :]
        o = _dot_nt(qe[r0:r0 + CHUNK], st.astype(BF16)) + _dot(attn, vc)
        st = st * jnp.exp(b_last) + _dot_tn(vc, kt)
        g = go_ref[0, r0:r0 + CHUNK, :].astype(F32)
        y = o * lax.rsqrt(jnp.mean(o * o, axis=-1, keepdims=True) + RMS_EPS) * gn
        o_ref[0, r0:r0 + CHUNK, :] = (y * (g / (1.0 + jnp.exp(-g)))).astype(BF16)

    st_ref[...] = st


def _gla_core(q, k, v, go, la, norm_gain, bsz, seq):
    tile = GLA_TILE
    kspec = pl.BlockSpec((1, tile, GLA_HK), lambda b, h, t: (b, t, h))
    vspec = pl.BlockSpec((1, tile, GLA_HV), lambda b, h, t: (b, t, h))
    out = pl.pallas_call(
        functools.partial(_gla_core_kernel, tile=tile),
        grid=(bsz, GLA_HEADS, seq // tile),
        in_specs=[kspec, kspec, vspec, vspec, kspec,
                  pl.BlockSpec((1, 1, GLA_HV), lambda b, h, t: (h, 0, 0))],
        out_specs=vspec,
        out_shape=jax.ShapeDtypeStruct((bsz, seq, GLA_VAL_DIM), BF16),
        scratch_shapes=[pltpu.VMEM((GLA_HV, GLA_HK), F32)],
        compiler_params=_params(("parallel", "parallel", "arbitrary")),
        name="gla_core",
    )(q.reshape(bsz, seq, GLA_KEY_DIM), k.reshape(bsz, seq, GLA_KEY_DIM), v.reshape(bsz, seq, GLA_VAL_DIM),
      go.reshape(bsz, seq, GLA_VAL_DIM), la.reshape(bsz, seq, GLA_KEY_DIM),
      norm_gain.astype(F32).reshape(GLA_HEADS, 1, GLA_HV))
    return out.reshape(bsz * seq, GLA_VAL_DIM)


def _pool_mixer(u, halo, first_tile_of_seq, pw_ref, ps_ref):
    halo = jnp.where(first_tile_of_seq, 0.0, halo)
    ext = jnp.concatenate([halo, u], axis=0)
    pos1 = lax.broadcasted_iota(jnp.int32, (POOL_HALO, POOL_GROUP_DIM), 0) + 1
    rs = []
    for g, w in enumerate(POOL_WINDOWS):
        cols = slice(g * POOL_GROUP_DIM, (g + 1) * POOL_GROUP_DIM)
        e = ext[:, cols]
        sh = 1
        while sh < w:
            e = e + pltpu.roll(e, sh, axis=0)
            sh *= 2
        head = e[POOL_HALO:2 * POOL_HALO] / jnp.minimum(pos1, w).astype(F32)
        mean = jnp.concatenate([jnp.where(first_tile_of_seq, head, e[POOL_HALO:2 * POOL_HALO] * (1.0 / w)),
                                e[2 * POOL_HALO:] * (1.0 / w)], axis=0)
        rs.append((mean - u[:, cols]).astype(BF16))
    ys = []
    for pair in range(2):
        r2 = jnp.concatenate(rs[2 * pair:2 * pair + 2], axis=1)
        ys.append(_dot(r2, pw_ref[pair]))
    return (jnp.concatenate(ys, axis=1) * ps_ref[...]).astype(BF16)


def _ffn(x, gf_ref, w1_ref, w2_ref):
    hb = _rms(x, gf_ref[...]).astype(BF16)
    d_ff = w1_ref.shape[1]
    step = 1024
    acc = x
    for c0 in range(0, d_ff, step):
        mid = jnp.maximum(_dot(hb, w1_ref[:, c0:c0 + step]), 0.0)
        acc = acc + _dot((mid * mid).astype(BF16), w2_ref[c0:c0 + step, :])
    return acc


def _pool_concat_kernel(a_ref, u_ref, uh_ref, pw_ref, ps_ref, o_ref, *, tiles_per_seq):
    it = pl.program_id(0) % tiles_per_seq
    o_ref[:, 0:DIFF_WIDTH] = a_ref[...]
    o_ref[:, DIFF_WIDTH:] = _pool_mixer(u_ref[...], uh_ref[...], it == 0, pw_ref, ps_ref)


def _mix_ffn_kernel(x_ref, y_ref, wo_ref, gf_ref, w1_ref, w2_ref, gl_ref, o_ref, *, final):
    x = x_ref[...] + _dot(y_ref[...], wo_ref[...])
    y = _ffn(x, gf_ref, w1_ref, w2_ref)
    o_ref[...] = _rms(y, gl_ref[...]) if final else y


def _pool_concat(a, u, pw_bd, pool_scale, seq):
    n = a.shape[0]
    tm = TOKEN_TILE
    tok = lambda width: pl.BlockSpec((tm, width), lambda i: (i, 0))
    halo_blocks = tm // POOL_HALO
    return pl.pallas_call(
        functools.partial(_pool_concat_kernel, tiles_per_seq=seq // tm),
        grid=(n // tm,),
        in_specs=[tok(DIFF_WIDTH), tok(POOL_WIDTH),
                  pl.BlockSpec((POOL_HALO, POOL_WIDTH), lambda i: (jnp.maximum(i * halo_blocks - 1, 0), 0)),
                  _const_spec(pw_bd.shape), _const_spec((1, POOL_WIDTH))],
        out_specs=tok(DIFF_WIDTH + POOL_WIDTH),
        out_shape=jax.ShapeDtypeStruct((n, DIFF_WIDTH + POOL_WIDTH), BF16),
        compiler_params=_params(("parallel",)),
        name="pool_concat",
    )(a, u, u, pw_bd, pool_scale.astype(F32).reshape(1, POOL_WIDTH))


def _mix_ffn(x2, y, w_out, gain_ffn, w1, w2, gain_last, final):
    n, d = x2.shape
    tm = TOKEN_TILE
    tok = lambda width: pl.BlockSpec((tm, width), lambda i: (i, 0))
    return pl.pallas_call(
        functools.partial(_mix_ffn_kernel, final=final),
        grid=(n // tm,),
        in_specs=[tok(d), tok(y.shape[1]), _const_spec(w_out.shape), _const_spec((1, d)), _const_spec(w1.shape),
                  _const_spec(w2.shape), _const_spec((1, d))],
        out_specs=tok(d),
        out_shape=jax.ShapeDtypeStruct((n, d), F32),
        compiler_params=_params(("parallel",)),
        name="mix_ffn",
    )(x2, y, w_out, gain_ffn.reshape(1, d), w1, w2, gain_last.reshape(1, d))


def _pool_block_diag(pool_w):
    g = POOL_GROUP_DIM
    z = jnp.zeros((g, g), pool_w.dtype)
    pairs = [jnp.block([[pool_w[2 * p], z], [z, pool_w[2 * p + 1]]]) for p in range(2)]
    return jnp.stack(pairs).astype(BF16)


def kernel(x, norm_mix, norm_ffn, norm_final, ab_w_in, ab_lambda, ab_subln, pool_w, pool_scale, ab_w_out, gla_w_in,
           gla_w_gk_up, gla_b_gk, gla_norm, gla_w_out, ffn_w1, ffn_w2):
    bsz, seq, d = x.shape
    depth = norm_mix.shape[0]
    x2 = x.reshape(bsz * seq, d).astype(F32)
    for i in range(depth):
        final = i == depth - 1
        w1 = ffn_w1[i].astype(BF16)
        w2 = ffn_w2[i].astype(BF16)
        if i % 2 == 0:
            e = i // 2
            q, k, vt, u = _ab_in_proj(x2, norm_mix[i], ab_w_in[e].astype(BF16), bsz, seq)
            a = _diff_attention(q, k, vt, ab_lambda[e], ab_subln[e], i, bsz, seq)
            y = _pool_concat(a, u, _pool_block_diag(pool_w[e]), pool_scale[e], seq)
            x2 = _mix_ffn(x2, y, ab_w_out[e].astype(BF16), norm_ffn[i], w1, w2, norm_final, final)
        else:
            o = i // 2
            split = 2 * GLA_KEY_DIM + 2 * GLA_VAL_DIM
            w_in = jnp.pad(gla_w_in[o], ((0, 0), (0, GLA_GATE_PAD - GLA_GATE_RANK))).astype(BF16)
            assert w_in.shape[1] == split + GLA_GATE_PAD
            w_up = jnp.pad(gla_w_gk_up[o], ((0, GLA_GATE_PAD - GLA_GATE_RANK), (0, 0))).astype(BF16)
            q, k, v, go, la = _gla_in_proj(x2, norm_mix[i], w_in, w_up, gla_b_gk[o])
            y = _gla_core(q, k, v, go, la, gla_norm[o], bsz, seq)
            x2 = _mix_ffn(x2, y, gla_w_out[o].astype(BF16), norm_ffn[i], w1, w2, norm_final, final)
    return x2.reshape(bsz, seq, d)
```

```python
import functools
import math

import jax
import jax.numpy as jnp
from jax import lax
from jax.experimental import pallas as pl
from jax.experimental.pallas import tpu as pltpu

F32 = jnp.float32
BF16 = jnp.bfloat16

RMS_EPS = 1e-6
CHUNK = 64

DIFF_HEADS = 4
DIFF_HEAD_DIM = 64
DIFF_V_DIM = 2 * DIFF_HEAD_DIM
DIFF_WIDTH = DIFF_HEADS * DIFF_V_DIM
POOL_WINDOWS = (2, 4, 8, 16)
POOL_GROUP_DIM = 128
POOL_WIDTH = POOL_GROUP_DIM * len(POOL_WINDOWS)
POOL_HALO = 16

GLA_HEADS = 4
GLA_HK = 128
GLA_HV = 256
GLA_KEY_DIM = GLA_HEADS * GLA_HK
GLA_VAL_DIM = GLA_HEADS * GLA_HV
GLA_GATE_RANK = 16
GLA_GATE_TEMP = 16.0
GLA_GATE_PAD = 128
GLA_SUB = 16
GLA_EXP_CAP = 60.0

VMEM_LIMIT_BYTES = 56 * 1024 * 1024
TOKEN_TILE = 512
ATTN_Q_TILE = 1024
ATTN_K_TILE = 512
GLA_TILE = 1024
ATTN_V_ROWS = DIFF_V_DIM + 16

LOG2E = 1.4426950408889634
NEG_BIG = -1e30


def _dot(a, b):
    return jnp.dot(a, b, preferred_element_type=F32)


def _dot_nt(a, b):
    return lax.dot_general(a, b, (((1,), (1,)), ((), ())), preferred_element_type=F32)


def _dot_tn(a, b):
    return lax.dot_general(a, b, (((0,), (0,)), ((), ())), preferred_element_type=F32)


def _rms(x, gain):
    return x * lax.rsqrt(jnp.mean(x * x, axis=-1, keepdims=True) + RMS_EPS) * gain


def _const_spec(shape):
    nd = len(shape)
    return pl.BlockSpec(shape, lambda *_: (0,) * nd, pipeline_mode=pl.Buffered(1))


def _layer_spec(stacked, layer):
    _, r, c = stacked.shape
    return pl.BlockSpec((None, r, c), lambda *_: (layer, 0, 0), pipeline_mode=pl.Buffered(1))


def _params(sem):
    return pltpu.CompilerParams(dimension_semantics=sem, vmem_limit_bytes=VMEM_LIMIT_BYTES)


def _pool_mixer(u, halo, first_tile_of_seq, pw_ref, ps_ref):
    ext = jnp.concatenate([halo, u], axis=0)
    pos1 = lax.broadcasted_iota(jnp.int32, (POOL_HALO, POOL_GROUP_DIM), 0) + 1
    rs = []
    for g, w in enumerate(POOL_WINDOWS):
        cols = slice(g * POOL_GROUP_DIM, (g + 1) * POOL_GROUP_DIM)
        e = ext[:, cols]
        sh = 1
        while sh < w:
            e = e + pltpu.roll(e, sh, axis=0)
            sh *= 2
        head = e[POOL_HALO:2 * POOL_HALO] / jnp.minimum(pos1, w).astype(F32)
        mean = jnp.concatenate([jnp.where(first_tile_of_seq, head, e[POOL_HALO:2 * POOL_HALO] * (1.0 / w)),
                                e[2 * POOL_HALO:] * (1.0 / w)], axis=0)
        rs.append((mean - u[:, cols]).astype(BF16))
    ys = []
    for pair in range(2):
        r2 = jnp.concatenate(rs[2 * pair:2 * pair + 2], axis=1)
        ys.append(_dot(r2, pw_ref[pair]))
    return (jnp.concatenate(ys, axis=1) * ps_ref[...]).astype(BF16)


def _ab_in_body(x, first_tile_of_seq, g_ref, w_ref, pw_ref, ps_ref, q_ref, k_ref, vt_ref, b_ref, halo_sc):
    hb = _rms(x, g_ref[...]).astype(BF16)
    w = DIFF_WIDTH
    q_ref[...] = (_dot(hb, w_ref[:, 0:w]) * (DIFF_HEAD_DIM ** -0.5 * LOG2E)).astype(BF16)
    k_ref[...] = _dot(hb, w_ref[:, w:2 * w]).astype(BF16)
    v = _dot(hb, w_ref[:, 2 * w:3 * w])
    tm = v.shape[0]
    pad_rows = lax.broadcasted_iota(jnp.int32, (ATTN_V_ROWS - DIFF_V_DIM, tm), 0)
    pad = jnp.where(pad_rows == 0, 1.0, 0.0).astype(BF16)
    for h in range(DIFF_HEADS):
        vt_ref[0, h, 0, 0:DIFF_V_DIM, :] = v[:, h * DIFF_V_DIM:(h + 1) * DIFF_V_DIM].T.astype(BF16)
        vt_ref[0, h, 0, DIFF_V_DIM:, :] = pad
    u = _dot(hb, w_ref[:, 3 * w:3 * w + POOL_WIDTH])

    @pl.when(first_tile_of_seq)
    def _():
        halo_sc[...] = jnp.zeros(halo_sc.shape, F32)

    b_ref[...] = _pool_mixer(u, halo_sc[...], first_tile_of_seq, pw_ref, ps_ref)
    halo_sc[...] = u[tm - POOL_HALO:, :]


def _ab_in_kernel(x_ref, *refs, tiles_per_seq):
    _ab_in_body(x_ref[...], pl.program_id(0) % tiles_per_seq == 0, *refs)


def _ab_in_specs(n, d, tm, nkb, bsz, w_in, pw_bd):
    tok = lambda width: pl.BlockSpec((tm, width), lambda i: (i, 0))
    in_specs = [_const_spec((1, d)), _const_spec(w_in.shape), _const_spec(pw_bd.shape), _const_spec((1, POOL_WIDTH))]
    out_specs = [tok(DIFF_WIDTH), tok(DIFF_WIDTH),
                 pl.BlockSpec((1, DIFF_HEADS, 1, ATTN_V_ROWS, tm), lambda i: (i // nkb, 0, i % nkb, 0, 0)),
                 tok(POOL_WIDTH)]
    out_shapes = [jax.ShapeDtypeStruct((n, DIFF_WIDTH), BF16), jax.ShapeDtypeStruct((n, DIFF_WIDTH), BF16),
                  jax.ShapeDtypeStruct((bsz, DIFF_HEADS, nkb, ATTN_V_ROWS, tm), BF16),
                  jax.ShapeDtypeStruct((n, POOL_WIDTH), BF16)]
    return in_specs, out_specs, out_shapes, [pltpu.VMEM((POOL_HALO, POOL_WIDTH), F32)]


def _ab_in_proj(x2, gain, w_in, pw_bd, pool_scale, bsz, seq):
    n, d = x2.shape
    tm = TOKEN_TILE
    assert tm == ATTN_K_TILE
    nkb = seq // tm
    in_specs, out_specs, out_shapes, scratch = _ab_in_specs(n, d, tm, nkb, bsz, w_in, pw_bd)
    return pl.pallas_call(
        functools.partial(_ab_in_kernel, tiles_per_seq=nkb),
        grid=(n // tm,),
        in_specs=[pl.BlockSpec((tm, d), lambda i: (i, 0))] + in_specs,
        out_specs=out_specs,
        out_shape=out_shapes,
        scratch_shapes=scratch,
        compiler_params=_params(("arbitrary",)),
        name="ab_in_proj",
    )(x2, gain.reshape(1, d), w_in, pw_bd, pool_scale.astype(F32).reshape(1, POOL_WIDTH))


def _attn_kernel(q_ref, k_ref, vt_ref, bias_ref, lam_ref, gain_ref, o_ref, m_sc, acc_sc, sa_sc, sb_sc, ma_sc, mb_sc, *,
                 tq, tk, lam_init):
    qi = pl.program_id(2)
    q = q_ref[0]
    lane = lax.broadcasted_iota(jnp.int32, q.shape, 1)
    zero = jnp.zeros_like(q)
    qz = jnp.concatenate([jnp.where(lane < DIFF_HEAD_DIM, q, zero), jnp.where(lane >= DIFF_HEAD_DIM, q, zero)], axis=0)

    m_sc[...] = jnp.full(m_sc.shape, NEG_BIG, F32)
    acc_sc[...] = jnp.zeros(acc_sc.shape, F32)

    def produce(i, s_ref, mx_ref, diagonal):
        s = _dot_nt(k_ref[0, pl.ds(pl.multiple_of(i * tk, tk), tk), :], qz)
        if diagonal is not None:
            s = s + bias_ref[diagonal]
        s_ref[...] = s
        mx_ref[...] = jnp.max(s, axis=0, keepdims=True)

    def consume(s_ref, mx_ref, i):
        m_old = m_sc[...]
        m_new = jnp.maximum(m_old, mx_ref[...])
        p = jnp.exp2(s_ref[...] - m_new).astype(BF16)
        acc_sc[...] = acc_sc[...] * jnp.exp2(m_old - m_new) + _dot(vt_ref[0, i], p)
        m_sc[...] = m_new

    @pl.when(qi > 0)
    def _():
        produce(0, sa_sc, ma_sc, None)

    @pl.when(qi == 0)
    def _():
        produce(0, sa_sc, ma_sc, 0)

    def pair(j, carry):
        i0 = 2 * j
        produce(i0 + 1, sb_sc, mb_sc, None)
        consume(sa_sc, ma_sc, i0)
        produce(i0 + 2, sa_sc, ma_sc, None)
        consume(sb_sc, mb_sc, i0 + 1)
        return carry

    lax.fori_loop(0, jnp.maximum(qi - 1, 0), pair, 0)

    @pl.when(qi > 0)
    def _():
        produce(2 * qi - 1, sb_sc, mb_sc, None)
        consume(sa_sc, ma_sc, 2 * qi - 2)
        produce(2 * qi, sa_sc, ma_sc, 0)
        consume(sb_sc, mb_sc, 2 * qi - 1)

    produce(2 * qi + 1, sb_sc, mb_sc, 1)
    consume(sa_sc, ma_sc, 2 * qi)
    consume(sb_sc, mb_sc, 2 * qi + 1)

    acc = acc_sc[...]
    o = acc[:DIFF_V_DIM, :] / acc[DIFF_V_DIM:DIFF_V_DIM + 1, :]
    lp = lam_ref[...]
    lam = (jnp.exp(jnp.sum(lp[0:1] * lp[1:2], axis=1, keepdims=True))
           - jnp.exp(jnp.sum(lp[2:3] * lp[3:4], axis=1, keepdims=True)) + lam_init)
    d = o[:, :tq] - lam * o[:, tq:]
    y = d * lax.rsqrt(jnp.mean(d * d, axis=0, keepdims=True) + RMS_EPS) * gain_ref[...] * (1.0 - lam_init)
    o_ref[0] = y.T.astype(BF16)


def _diff_attention(q, k, vt, lam_params, subln_gain, layer_idx, bsz, seq):
    tq, tk = ATTN_Q_TILE, ATTN_K_TILE
    assert tq == 2 * tk
    nkb = seq // tk
    lam_init = 0.8 - 0.6 * math.exp(-0.3 * layer_idx)
    q3 = q.reshape(bsz, seq, DIFF_WIDTH)
    k3 = k.reshape(bsz, seq, DIFF_WIDTH)
    vt = vt.reshape(bsz * DIFF_HEADS, nkb, ATTN_V_ROWS, tk)
    gain_b = jnp.broadcast_to(subln_gain.astype(F32).reshape(DIFF_V_DIM, 1), (DIFF_V_DIM, tq))
    key_chunk = jnp.arange(tq).reshape(2, tk) // CHUNK
    qry_chunk = jnp.arange(tq) // CHUNK
    bias = jnp.where(key_chunk[:, :, None] <= qry_chunk[None, None, :], 0.0, NEG_BIG).astype(F32)
    bias = jnp.concatenate([bias, bias], axis=2)
    out = pl.pallas_call(
        functools.partial(_attn_kernel, tq=tq, tk=tk, lam_init=lam_init),
        grid=(bsz, DIFF_HEADS, seq // tq),
        in_specs=[
            pl.BlockSpec((1, tq, DIFF_V_DIM), lambda b, h, i: (b, i, h)),
            pl.BlockSpec((1, seq, DIFF_V_DIM), lambda b, h, i: (b, 0, h)),
            pl.BlockSpec((1, nkb, ATTN_V_ROWS, tk), lambda b, h, i: (b * DIFF_HEADS + h, 0, 0, 0)),
            _const_spec((2, tk, 2 * tq)),
            _const_spec((4, DIFF_HEAD_DIM)),
            _const_spec((DIFF_V_DIM, tq)),
        ],
        out_specs=pl.BlockSpec((1, tq, DIFF_V_DIM), lambda b, h, i: (b, i, h)),
        out_shape=jax.ShapeDtypeStruct((bsz, seq, DIFF_WIDTH), BF16),
        scratch_shapes=[pltpu.VMEM((1, 2 * tq), F32), pltpu.VMEM((ATTN_V_ROWS, 2 * tq), F32),
                        pltpu.VMEM((tk, 2 * tq), F32), pltpu.VMEM((tk, 2 * tq), F32),
                        pltpu.VMEM((1, 2 * tq), F32), pltpu.VMEM((1, 2 * tq), F32)],
        compiler_params=_params(("parallel", "parallel", "arbitrary")),
        name="diff_attention",
    )(q3, k3, vt, bias, lam_params.astype(F32), gain_b)
    return out.reshape(bsz * seq, DIFF_WIDTH)


def _gla_in_body(x, g_ref, w_ref, wup_ref, bgk_ref, q_ref, k_ref, v_ref, go_ref, la_ref):
    hb = _rms(x, g_ref[...]).astype(BF16)
    kd, vd = GLA_KEY_DIM, GLA_VAL_DIM
    q_ref[...] = (_dot(hb, w_ref[:, 0:kd]) * (GLA_HK ** -0.5)).astype(BF16)
    k_ref[...] = _dot(hb, w_ref[:, kd:2 * kd]).astype(BF16)
    v_ref[...] = _dot(hb, w_ref[:, 2 * kd:2 * kd + vd]).astype(BF16)
    go_ref[...] = _dot(hb, w_ref[:, 2 * kd + vd:2 * kd + 2 * vd]).astype(BF16)
    low = _dot(hb, w_ref[:, 2 * kd + 2 * vd:2 * kd + 2 * vd + GLA_GATE_PAD])
    z = _dot(low.astype(BF16), wup_ref[...]) + bgk_ref[...]
    la_ref[...] = (jnp.minimum(z, 0.0) - jnp.log(1.0 + jnp.exp(-jnp.abs(z)))) * (1.0 / GLA_GATE_TEMP)


def _gla_in_specs(n, d, tm, w_in_pad, w_up_pad):
    tok = lambda width: pl.BlockSpec((tm, width), lambda i: (i, 0))
    in_specs = [_const_spec((1, d)), _const_spec(w_in_pad.shape), _const_spec(w_up_pad.shape),
                _const_spec((1, GLA_KEY_DIM))]
    out_specs = [tok(GLA_KEY_DIM), tok(GLA_KEY_DIM), tok(GLA_VAL_DIM), tok(GLA_VAL_DIM), tok(GLA_KEY_DIM)]
    out_shapes = [jax.ShapeDtypeStruct((n, GLA_KEY_DIM), BF16), jax.ShapeDtypeStruct((n, GLA_KEY_DIM), BF16),
                  jax.ShapeDtypeStruct((n, GLA_VAL_DIM), BF16), jax.ShapeDtypeStruct((n, GLA_VAL_DIM), BF16),
                  jax.ShapeDtypeStruct((n, GLA_KEY_DIM), F32)]
    return in_specs, out_specs, out_shapes


def _gla_core_kernel(q_ref, k_ref, v_ref, go_ref, la_ref, gn_ref, o_ref, st_ref, *, tile):
    @pl.when(pl.program_id(2) == 0)
    def _():
        st_ref[...] = jnp.zeros(st_ref.shape, F32)

    rin = lax.broadcasted_iota(jnp.int32, (tile, GLA_HK), 0) & (CHUNK - 1)
    b = la_ref[0]
    sh = 1
    while sh < CHUNK:
        b = b + jnp.where(rin >= sh, pltpu.roll(b, sh, axis=0), 0.0)
        sh *= 2

    q = q_ref[0].astype(F32)
    k = k_ref[0].astype(F32)
    qe = (q * jnp.exp(b)).astype(BF16)
    causal = (lax.broadcasted_iota(jnp.int32, (CHUNK, CHUNK), 0)
              >= lax.broadcasted_iota(jnp.int32, (CHUNK, CHUNK), 1))
    gn = gn_ref[0]
    st = st_ref[...]

    for c in range(tile // CHUNK):
        r0 = c * CHUNK
        bc = b[r0:r0 + CHUNK]
        qc = q[r0:r0 + CHUNK]
        kc = k[r0:r0 + CHUNK]
        b_last = bc[CHUNK - 1:CHUNK]
        kt = (kc * jnp.exp(b_last - bc)).astype(BF16)
        rows = []
        for a in range(CHUNK // GLA_SUB):
            s0 = a * GLA_SUB
            bref = bc[s0:s0 + 1]
            qa = (qc[s0:s0 + GLA_SUB] * jnp.exp(bc[s0:s0 + GLA_SUB] - bref)).astype(BF16)
            ka = (kc * jnp.exp(jnp.minimum(bref - bc, GLA_EXP_CAP))).astype(BF16)
            rows.append(_dot_nt(qa, ka))
        attn = jnp.where(causal, jnp.concatenate(rows, axis=0), 0.0).astype(BF16)
        vc = v_ref[0, r0:r0 + CHUNK, :]
        o = _dot_nt(qe[r0:r0 + CHUNK], st.astype(BF16)) + _dot(attn, vc)
        st = st * jnp.exp(b_last) + _dot_tn(vc, kt)
        g = go_ref[0, r0:r0 + CHUNK, :].astype(F32)
        y = o * lax.rsqrt(jnp.mean(o * o, axis=-1, keepdims=True) + RMS_EPS) * gn
        o_ref[0, r0:r0 + CHUNK, :] = (y * (g / (1.0 + jnp.exp(-g)))).astype(BF16)

    st_ref[...] = st


def _gla_core(q, k, v, go, la, norm_gain, bsz, seq):
    tile = GLA_TILE
    kspec = pl.BlockSpec((1, tile, GLA_HK), lambda b, h, t: (b, t, h))
    vspec = pl.BlockSpec((1, tile, GLA_HV), lambda b, h, t: (b, t, h))
    out = pl.pallas_call(
        functools.partial(_gla_core_kernel, tile=tile),
        grid=(bsz, GLA_HEADS, seq // tile),
        in_specs=[kspec, kspec, vspec, vspec, kspec,
                  pl.BlockSpec((1, 1, GLA_HV), lambda b, h, t: (h, 0, 0))],
        out_specs=vspec,
        out_shape=jax.ShapeDtypeStruct((bsz, seq, GLA_VAL_DIM), BF16),
        scratch_shapes=[pltpu.VMEM((GLA_HV, GLA_HK), F32)],
        compiler_params=_params(("parallel", "parallel", "arbitrary")),
        name="gla_core",
    )(q.reshape(bsz, seq, GLA_KEY_DIM), k.reshape(bsz, seq, GLA_KEY_DIM), v.reshape(bsz, seq, GLA_VAL_DIM),
      go.reshape(bsz, seq, GLA_VAL_DIM), la.reshape(bsz, seq, GLA_KEY_DIM),
      norm_gain.astype(F32).reshape(GLA_HEADS, 1, GLA_HV))
    return out.reshape(bsz * seq, GLA_VAL_DIM)


def _ffn(x, gf_ref, w1_ref, w2_ref):
    hb = _rms(x, gf_ref[...]).astype(BF16)
    d_ff = w1_ref.shape[1]
    step = 1024
    acc = x
    for c0 in range(0, d_ff, step):
        mid = jnp.maximum(_dot(hb, w1_ref[:, c0:c0 + step]), 0.0)
        acc = acc + _dot((mid * mid).astype(BF16), w2_ref[c0:c0 + step, :])
    return acc


def _mix_ffn_kernel(*refs, n_mix, nxt, final, tiles_per_seq):
    x_ref, y_refs = refs[0], refs[1:1 + n_mix]
    wo_ref, gf_ref, w1_ref, w2_ref, gl_ref = refs[1 + n_mix:6 + n_mix]
    n_nxt = 4 if nxt is not None else 0
    nxt_in = refs[6 + n_mix:6 + n_mix + n_nxt]
    o_ref = refs[6 + n_mix + n_nxt]
    nxt_out = refs[7 + n_mix + n_nxt:]
    mixed = y_refs[0][...] if n_mix == 1 else jnp.concatenate([r[...] for r in y_refs], axis=1)
    x = x_ref[...] + _dot(mixed, wo_ref[...])
    y = _ffn(x, gf_ref, w1_ref, w2_ref)
    o_ref[...] = _rms(y, gl_ref[...]) if final else y
    if nxt == "ab":
        _ab_in_body(y, pl.program_id(0) % tiles_per_seq == 0, *nxt_in, *nxt_out)
    elif nxt == "gla":
        _gla_in_body(y, *nxt_in, *nxt_out)


def _mix_ffn(x2, ys, w_out, w_out_layer, gain_ffn, w1, w2, ffn_layer, gain_last, final, bsz, seq, nxt=None,
             nxt_args=()):
    n, d = x2.shape
    tm = TOKEN_TILE
    tok = lambda width: pl.BlockSpec((tm, width), lambda i: (i, 0))
    in_specs = [tok(d)] + [tok(y.shape[1]) for y in ys] + [_layer_spec(w_out, w_out_layer), _const_spec((1, d)),
                                                           _layer_spec(w1, ffn_layer), _layer_spec(w2, ffn_layer),
                                                           _const_spec((1, d))]
    out_specs, out_shapes, scratch = [tok(d)], [jax.ShapeDtypeStruct((n, d), F32)], []
    if nxt == "ab":
        gain_n, w_in, pw_bd, pool_scale = nxt_args
        ni, no, ns, scratch = _ab_in_specs(n, d, tm, seq // tm, bsz, w_in, pw_bd)
        nxt_ops = (gain_n.reshape(1, d), w_in, pw_bd, pool_scale.astype(F32).reshape(1, POOL_WIDTH))
    elif nxt == "gla":
        gain_n, w_in_pad, w_up_pad, b_gk = nxt_args
        ni, no, ns = _gla_in_specs(n, d, tm, w_in_pad, w_up_pad)
        nxt_ops = (gain_n.reshape(1, d), w_in_pad, w_up_pad, b_gk.reshape(1, GLA_KEY_DIM))
    else:
        ni, no, ns, nxt_ops = [], [], [], ()
    return pl.pallas_call(
        functools.partial(_mix_ffn_kernel, n_mix=len(ys), nxt=nxt, final=final, tiles_per_seq=seq // tm),
        grid=(n // tm,),
        in_specs=in_specs + ni,
        out_specs=out_specs + no,
        out_shape=out_shapes + ns,
        scratch_shapes=scratch,
        compiler_params=_params(("arbitrary",)),
        name="mix_ffn",
    )(x2, *ys, w_out, gain_ffn.reshape(1, d), w1, w2, gain_last.reshape(1, d), *nxt_ops)


def _pool_block_diag(pool_w):
    g = POOL_GROUP_DIM
    z = jnp.zeros((g, g), pool_w.dtype)
    pairs = [jnp.block([[pool_w[2 * p], z], [z, pool_w[2 * p + 1]]]) for p in range(2)]
    return jnp.stack(pairs).astype(BF16)


def kernel(x, norm_mix, norm_ffn, norm_final, ab_w_in, ab_lambda, ab_subln, pool_w, pool_scale, ab_w_out, gla_w_in,
           gla_w_gk_up, gla_b_gk, gla_norm, gla_w_out, ffn_w1, ffn_w2):
    bsz, seq, d = x.shape
    depth = norm_mix.shape[0]
    x2 = x.reshape(bsz * seq, d).astype(F32)
    w1_all, w2_all = ffn_w1.astype(BF16), ffn_w2.astype(BF16)
    ab_wo_all, gla_wo_all = ab_w_out.astype(BF16), gla_w_out.astype(BF16)

    def in_args(i):
        if i % 2 == 0:
            e = i // 2
            return "ab", (norm_mix[i], ab_w_in[e].astype(BF16), _pool_block_diag(pool_w[e]), pool_scale[e])
        o = i // 2
        w_in = jnp.pad(gla_w_in[o], ((0, 0), (0, GLA_GATE_PAD - GLA_GATE_RANK))).astype(BF16)
        assert w_in.shape[1] == 2 * GLA_KEY_DIM + 2 * GLA_VAL_DIM + GLA_GATE_PAD
        w_up = jnp.pad(gla_w_gk_up[o], ((0, GLA_GATE_PAD - GLA_GATE_RANK), (0, 0))).astype(BF16)
        return "gla", (norm_mix[i], w_in, w_up, gla_b_gk[o])

    mix_in = _ab_in_proj(x2, *in_args(0)[1], bsz, seq)
    for i in range(depth):
        final = i == depth - 1
        nxt, nxt_args = (None, ()) if final else in_args(i + 1)
        if i % 2 == 0:
            e = i // 2
            q, k, vt, b = mix_in
            a = _diff_attention(q, k, vt, ab_lambda[e], ab_subln[e], i, bsz, seq)
            ys, wo_all, wo_layer = [a, b], ab_wo_all, e
        else:
            o = i // 2
            q, k, v, go, la = mix_in
            ys, wo_all, wo_layer = [_gla_core(q, k, v, go, la, gla_norm[o], bsz, seq)], gla_wo_all, o
        outs = _mix_ffn(x2, ys, wo_all, wo_layer, norm_ffn[i], w1_all, w2_all, i, norm_final, final, bsz, seq, nxt,
                        nxt_args)
        x2, mix_in = outs[0], outs[1:]
    return x2.reshape(bsz, seq, d)
```

```python
import functools
import math

import jax
import jax.numpy as jnp
from jax import lax
from jax.experimental import pallas as pl
from jax.experimental.pallas import tpu as pltpu

F32 = jnp.float32
BF16 = jnp.bfloat16

RMS_EPS = 1e-6
CHUNK = 64

DIFF_HEADS = 4
DIFF_HEAD_DIM = 64
DIFF_V_DIM = 2 * DIFF_HEAD_DIM
DIFF_WIDTH = DIFF_HEADS * DIFF_V_DIM
POOL_WINDOWS = (2, 4, 8, 16)
POOL_GROUP_DIM = 128
POOL_WIDTH = POOL_GROUP_DIM * len(POOL_WINDOWS)
POOL_HALO = 16

GLA_HEADS = 4
GLA_HK = 128
GLA_HV = 256
GLA_KEY_DIM = GLA_HEADS * GLA_HK
GLA_VAL_DIM = GLA_HEADS * GLA_HV
GLA_GATE_RANK = 16
GLA_GATE_TEMP = 16.0
GLA_GATE_PAD = 128
GLA_SUB = 16
GLA_EXP_CAP = 60.0

VMEM_LIMIT_BYTES = 56 * 1024 * 1024
TOKEN_TILE = 512
ATTN_Q_TILE = 1024
ATTN_K_TILE = 512
GLA_TILE = 1024
ATTN_V_ROWS = DIFF_V_DIM + 16

LOG2E = 1.4426950408889634
NEG_BIG = -1e30


def _dot(a, b):
    return jnp.dot(a, b, preferred_element_type=F32)


def _dot_nt(a, b):
    return lax.dot_general(a, b, (((1,), (1,)), ((), ())), preferred_element_type=F32)


def _dot_tn(a, b):
    return lax.dot_general(a, b, (((0,), (0,)), ((), ())), preferred_element_type=F32)


def _rms(x, gain):
    return x * lax.rsqrt(jnp.mean(x * x, axis=-1, keepdims=True) + RMS_EPS) * gain


def _const_spec(shape):
    nd = len(shape)
    return pl.BlockSpec(shape, lambda *_: (0,) * nd, pipeline_mode=pl.Buffered(1))


def _layer_spec(stacked, layer):
    _, r, c = stacked.shape
    return pl.BlockSpec((None, r, c), lambda *_: (layer, 0, 0), pipeline_mode=pl.Buffered(1))


def _params(sem):
    return pltpu.CompilerParams(dimension_semantics=sem, vmem_limit_bytes=VMEM_LIMIT_BYTES)


def _pool_mixer(u, halo, first_tile_of_seq, pw_ref, ps_ref):
    ext = jnp.concatenate([halo, u], axis=0)
    pos1 = lax.broadcasted_iota(jnp.int32, (POOL_HALO, POOL_GROUP_DIM), 0) + 1
    rs = []
    for g, w in enumerate(POOL_WINDOWS):
        cols = slice(g * POOL_GROUP_DIM, (g + 1) * POOL_GROUP_DIM)
        e = ext[:, cols]
        sh = 1
        while sh < w:
            e = e + pltpu.roll(e, sh, axis=0)
            sh *= 2
        head = e[POOL_HALO:2 * POOL_HALO] / jnp.minimum(pos1, w).astype(F32)
        mean = jnp.concatenate([jnp.where(first_tile_of_seq, head, e[POOL_HALO:2 * POOL_HALO] * (1.0 / w)),
                                e[2 * POOL_HALO:] * (1.0 / w)], axis=0)
        rs.append((mean - u[:, cols]).astype(BF16))
    ys = []
    for pair in range(2):
        r2 = jnp.concatenate(rs[2 * pair:2 * pair + 2], axis=1)
        ys.append(_dot(r2, pw_ref[pair]))
    return (jnp.concatenate(ys, axis=1) * ps_ref[...]).astype(BF16)


def _ab_in_body(x, first_tile_of_seq, g_ref, w_ref, pw_ref, ps_ref, q_ref, k_ref, vt_ref, b_ref, halo_sc):
    hb = _rms(x, g_ref[...]).astype(BF16)
    w = DIFF_WIDTH
    q_ref[...] = (_dot(hb, w_ref[:, 0:w]) * (DIFF_HEAD_DIM ** -0.5 * LOG2E)).astype(BF16)
    k_ref[...] = _dot(hb, w_ref[:, w:2 * w]).astype(BF16)
    v = _dot(hb, w_ref[:, 2 * w:3 * w])
    tm = v.shape[0]
    pad_rows = lax.broadcasted_iota(jnp.int32, (ATTN_V_ROWS - DIFF_V_DIM, tm), 0)
    pad = jnp.where(pad_rows == 0, 1.0, 0.0).astype(BF16)
    for h in range(DIFF_HEADS):
        vt_ref[0, h, 0, 0:DIFF_V_DIM, :] = v[:, h * DIFF_V_DIM:(h + 1) * DIFF_V_DIM].T.astype(BF16)
        vt_ref[0, h, 0, DIFF_V_DIM:, :] = pad
    u = _dot(hb, w_ref[:, 3 * w:3 * w + POOL_WIDTH])

    @pl.when(first_tile_of_seq)
    def _():
        halo_sc[...] = jnp.zeros(halo_sc.shape, F32)

    b_ref[...] = _pool_mixer(u, halo_sc[...], first_tile_of_seq, pw_ref, ps_ref)
    halo_sc[...] = u[tm - POOL_HALO:, :]


def _ab_in_kernel(x_ref, *refs, tiles_per_seq):
    _ab_in_body(x_ref[...], pl.program_id(0) % tiles_per_seq == 0, *refs)


def _ab_in_specs(n, d, tm, nkb, bsz, w_in, pw_bd):
    tok = lambda width: pl.BlockSpec((tm, width), lambda i: (i, 0))
    in_specs = [_const_spec((1, d)), _const_spec(w_in.shape), _const_spec(pw_bd.shape), _const_spec((1, POOL_WIDTH))]
    out_specs = [tok(DIFF_WIDTH), tok(DIFF_WIDTH),
                 pl.BlockSpec((1, DIFF_HEADS, 1, ATTN_V_ROWS, tm), lambda i: (i // nkb, 0, i % nkb, 0, 0)),
                 tok(POOL_WIDTH)]
    out_shapes = [jax.ShapeDtypeStruct((n, DIFF_WIDTH), BF16), jax.ShapeDtypeStruct((n, DIFF_WIDTH), BF16),
                  jax.ShapeDtypeStruct((bsz, DIFF_HEADS, nkb, ATTN_V_ROWS, tm), BF16),
                  jax.ShapeDtypeStruct((n, POOL_WIDTH), BF16)]
    return in_specs, out_specs, out_shapes, [pltpu.VMEM((POOL_HALO, POOL_WIDTH), F32)]


def _ab_in_proj(x2, gain, w_in, pw_bd, pool_scale, bsz, seq):
    n, d = x2.shape
    tm = TOKEN_TILE
    assert tm == ATTN_K_TILE
    nkb = seq // tm
    in_specs, out_specs, out_shapes, scratch = _ab_in_specs(n, d, tm, nkb, bsz, w_in, pw_bd)
    return pl.pallas_call(
        functools.partial(_ab_in_kernel, tiles_per_seq=nkb),
        grid=(n // tm,),
        in_specs=[pl.BlockSpec((tm, d), lambda i: (i, 0))] + in_specs,
        out_specs=out_specs,
        out_shape=out_shapes,
        scratch_shapes=scratch,
        compiler_params=_params(("arbitrary",)),
        name="ab_in_proj",
    )(x2, gain.reshape(1, d), w_in, pw_bd, pool_scale.astype(F32).reshape(1, POOL_WIDTH))


def _attn_kernel(q_ref, k_ref, vt_ref, bias0_ref, bias1_ref, lam_ref, gain_ref, o_ref, m_sc, acc_sc, sa_sc, sb_sc, ma_sc,
                 mb_sc, *, tq, tk, lam_init):
    qi = pl.program_id(2)
    q = q_ref[0]
    lane = lax.broadcasted_iota(jnp.int32, q.shape, 1)
    zero = jnp.zeros_like(q)
    qz = jnp.concatenate([jnp.where(lane < DIFF_HEAD_DIM, q, zero), jnp.where(lane >= DIFF_HEAD_DIM, q, zero)], axis=0)

    m_sc[...] = jnp.full(m_sc.shape, NEG_BIG, F32)
    acc_sc[...] = jnp.zeros(acc_sc.shape, F32)

    def produce(i, s_ref, mx_ref, diagonal):
        s = _dot_nt(k_ref[0, pl.ds(pl.multiple_of(i * tk, tk), tk), :], qz)
        if diagonal:
            s = s + bias0_ref[...]
        s_ref[...] = s
        mx_ref[...] = jnp.max(s, axis=0, keepdims=True)

    def consume(s_ref, mx_ref, i):
        m_old = m_sc[...]
        m_new = jnp.maximum(m_old, mx_ref[...])
        p = jnp.exp2(s_ref[...] - m_new).astype(BF16)
        acc_sc[...] = acc_sc[...] * jnp.exp2(m_old - m_new) + _dot(vt_ref[0, i], p)
        m_sc[...] = m_new

    @pl.when(qi > 0)
    def _():
        produce(0, sa_sc, ma_sc, False)

    @pl.when(qi == 0)
    def _():
        produce(0, sa_sc, ma_sc, True)

    def pair(j, carry):
        i0 = 2 * j
        produce(i0 + 1, sb_sc, mb_sc, False)
        consume(sa_sc, ma_sc, i0)
        produce(i0 + 2, sa_sc, ma_sc, False)
        consume(sb_sc, mb_sc, i0 + 1)
        return carry

    lax.fori_loop(0, jnp.maximum(qi - 1, 0), pair, 0)

    @pl.when(qi > 0)
    def _():
        produce(2 * qi - 1, sb_sc, mb_sc, False)
        consume(sa_sc, ma_sc, 2 * qi - 2)
        produce(2 * qi, sa_sc, ma_sc, True)
        consume(sb_sc, mb_sc, 2 * qi - 1)

    half = tq // 2
    vis = (slice(half, tq), slice(tq + half, 2 * tq))
    qz_vis = jnp.concatenate([qz[vis[0]], qz[vis[1]]], axis=0)
    i_last = 2 * qi + 1
    s_vis = _dot_nt(k_ref[0, pl.ds(pl.multiple_of(i_last * tk, tk), tk), :], qz_vis) + bias1_ref[...]
    sb_sc[:, 0:tq] = s_vis
    mb_sc[:, 0:tq] = jnp.max(s_vis, axis=0, keepdims=True)
    consume(sa_sc, ma_sc, 2 * qi)
    m_old = jnp.concatenate([m_sc[:, vis[0]], m_sc[:, vis[1]]], axis=1)
    m_new = jnp.maximum(m_old, mb_sc[:, 0:tq])
    p = jnp.exp2(sb_sc[:, 0:tq] - m_new).astype(BF16)
    alpha = jnp.exp2(m_old - m_new)
    pv = _dot(vt_ref[0, i_last], p)
    for n, cols in enumerate(vis):
        part = slice(n * half, (n + 1) * half)
        acc_sc[:, cols] = acc_sc[:, cols] * alpha[:, part] + pv[:, part]
        m_sc[:, cols] = m_new[:, part]

    acc = acc_sc[...]
    o = acc[:DIFF_V_DIM, :] / acc[DIFF_V_DIM:DIFF_V_DIM + 1, :]
    lp = lam_ref[...]
    lam = (jnp.exp(jnp.sum(lp[0:1] * lp[1:2], axis=1, keepdims=True))
           - jnp.exp(jnp.sum(lp[2:3] * lp[3:4], axis=1, keepdims=True)) + lam_init)
    d = o[:, :tq] - lam * o[:, tq:]
    y = d * lax.rsqrt(jnp.mean(d * d, axis=0, keepdims=True) + RMS_EPS) * gain_ref[...] * (1.0 - lam_init)
    o_ref[0] = y.T.astype(BF16)


def _diff_attention(q, k, vt, lam_params, subln_gain, layer_idx, bsz, seq):
    tq, tk = ATTN_Q_TILE, ATTN_K_TILE
    assert tq == 2 * tk
    nkb = seq // tk
    lam_init = 0.8 - 0.6 * math.exp(-0.3 * layer_idx)
    q3 = q.reshape(bsz, seq, DIFF_WIDTH)
    k3 = k.reshape(bsz, seq, DIFF_WIDTH)
    vt = vt.reshape(bsz * DIFF_HEADS, nkb, ATTN_V_ROWS, tk)
    gain_b = jnp.broadcast_to(subln_gain.astype(F32).reshape(DIFF_V_DIM, 1), (DIFF_V_DIM, tq))
    key_chunk = jnp.arange(tq).reshape(2, tk) // CHUNK
    qry_chunk = jnp.arange(tq) // CHUNK
    bias = jnp.where(key_chunk[:, :, None] <= qry_chunk[None, None, :], 0.0, NEG_BIG).astype(F32)
    bias0 = jnp.concatenate([bias[0], bias[0]], axis=1)
    bias1 = jnp.concatenate([bias[1][:, tq // 2:], bias[1][:, tq // 2:]], axis=1)
    out = pl.pallas_call(
        functools.partial(_attn_kernel, tq=tq, tk=tk, lam_init=lam_init),
        grid=(bsz, DIFF_HEADS, seq // tq),
        in_specs=[
            pl.BlockSpec((1, tq, DIFF_V_DIM), lambda b, h, i: (b, i, h)),
            pl.BlockSpec((1, seq, DIFF_V_DIM), lambda b, h, i: (b, 0, h)),
            pl.BlockSpec((1, nkb, ATTN_V_ROWS, tk), lambda b, h, i: (b * DIFF_HEADS + h, 0, 0, 0)),
            _const_spec((tk, 2 * tq)),
            _const_spec((tk, tq)),
            _const_spec((4, DIFF_HEAD_DIM)),
            _const_spec((DIFF_V_DIM, tq)),
        ],
        out_specs=pl.BlockSpec((1, tq, DIFF_V_DIM), lambda b, h, i: (b, i, h)),
        out_shape=jax.ShapeDtypeStruct((bsz, seq, DIFF_WIDTH), BF16),
        scratch_shapes=[pltpu.VMEM((1, 2 * tq), F32), pltpu.VMEM((ATTN_V_ROWS, 2 * tq), F32),
                        pltpu.VMEM((tk, 2 * tq), F32), pltpu.VMEM((tk, 2 * tq), F32),
                        pltpu.VMEM((1, 2 * tq), F32), pltpu.VMEM((1, 2 * tq), F32)],
        compiler_params=_params(("parallel", "parallel", "arbitrary")),
        name="diff_attention",
    )(q3, k3, vt, bias0, bias1, lam_params.astype(F32), gain_b)
    return out.reshape(bsz * seq, DIFF_WIDTH)


def _gla_in_body(x, g_ref, w_ref, wup_ref, bgk_ref, q_ref, k_ref, v_ref, go_ref, la_ref):
    hb = _rms(x, g_ref[...]).astype(BF16)
    kd, vd = GLA_KEY_DIM, GLA_VAL_DIM
    q_ref[...] = (_dot(hb, w_ref[:, 0:kd]) * (GLA_HK ** -0.5)).astype(BF16)
    k_ref[...] = _dot(hb, w_ref[:, kd:2 * kd]).astype(BF16)
    v_ref[...] = _dot(hb, w_ref[:, 2 * kd:2 * kd + vd]).astype(BF16)
    go_ref[...] = _dot(hb, w_ref[:, 2 * kd + vd:2 * kd + 2 * vd]).astype(BF16)
    low = _dot(hb, w_ref[:, 2 * kd + 2 * vd:2 * kd + 2 * vd + GLA_GATE_PAD])
    z = _dot(low.astype(BF16), wup_ref[...]) + bgk_ref[...]
    la_ref[...] = (jnp.minimum(z, 0.0) - jnp.log(1.0 + jnp.exp(-jnp.abs(z)))) * (1.0 / GLA_GATE_TEMP)


def _gla_in_specs(n, d, tm, w_in_pad, w_up_pad):
    tok = lambda width: pl.BlockSpec((tm, width), lambda i: (i, 0))
    in_specs = [_const_spec((1, d)), _const_spec(w_in_pad.shape), _const_spec(w_up_pad.shape),
                _const_spec((1, GLA_KEY_DIM))]
    out_specs = [tok(GLA_KEY_DIM), tok(GLA_KEY_DIM), tok(GLA_VAL_DIM), tok(GLA_VAL_DIM), tok(GLA_KEY_DIM)]
    out_shapes = [jax.ShapeDtypeStruct((n, GLA_KEY_DIM), BF16), jax.ShapeDtypeStruct((n, GLA_KEY_DIM), BF16),
                  jax.ShapeDtypeStruct((n, GLA_VAL_DIM), BF16), jax.ShapeDtypeStruct((n, GLA_VAL_DIM), BF16),
                  jax.ShapeDtypeStruct((n, GLA_KEY_DIM), F32)]
    return in_specs, out_specs, out_shapes


def _gla_core_kernel(q_ref, k_ref, v_ref, go_ref, la_ref, gn_ref, o_ref, st_ref, *, tile):
    @pl.when(pl.program_id(2) == 0)
    def _():
        st_ref[...] = jnp.zeros(st_ref.shape, F32)

    rin = lax.broadcasted_iota(jnp.int32, (tile, GLA_HK), 0) & (CHUNK - 1)
    b = la_ref[0]
    sh = 1
    while sh < CHUNK:
        b = b + jnp.where(rin >= sh, pltpu.roll(b, sh, axis=0), 0.0)
        sh *= 2

    q = q_ref[0].astype(F32)
    k = k_ref[0].astype(F32)
    qe = (q * jnp.exp(b)).astype(BF16)
    causal = (lax.broadcasted_iota(jnp.int32, (CHUNK, CHUNK), 0)
              >= lax.broadcasted_iota(jnp.int32, (CHUNK, CHUNK), 1))
    gn = gn_ref[0]
    st = st_ref[...]

    for c in range(tile // CHUNK):
        r0 = c * CHUNK
        bc = b[r0:r0 + CHUNK]
        qc = q[r0:r0 + CHUNK]
        kc = k[r0:r0 + CHUNK]
        b_last = bc[CHUNK - 1:CHUNK]
        kt = (kc * jnp.exp(b_last - bc)).astype(BF16)
        rows = []
        for a in range(CHUNK // GLA_SUB):
            s0 = a * GLA_SUB
            bref = bc[s0:s0 + 1]
            qa = (qc[s0:s0 + GLA_SUB] * jnp.exp(bc[s0:s0 + GLA_SUB] - bref)).astype(BF16)
            ka = (kc * jnp.exp(jnp.minimum(bref - bc, GLA_EXP_CAP))).astype(BF16)
            rows.append(_dot_nt(qa, ka))
        attn = jnp.where(causal, jnp.concatenate(rows, axis=0), 0.0).astype(BF16)
        vc = v_ref[0, r0:r0 + CHUNK, :]
        o = _dot_nt(qe[r0:r0 + CHUNK], st.astype(BF16)) + _dot(attn, vc)
        st = st * jnp.exp(b_last) + _dot_tn(vc, kt)
        g = go_ref[0, r0:r0 + CHUNK, :].astype(F32)
        y = o * lax.rsqrt(jnp.mean(o * o, axis=-1, keepdims=True) + RMS_EPS) * gn
        o_ref[0, r0:r0 + CHUNK, :] = (y * (g / (1.0 + jnp.exp(-g)))).astype(BF16)

    st_ref[...] = st


def _gla_core(q, k, v, go, la, norm_gain, bsz, seq):
    tile = GLA_TILE
    kspec = pl.BlockSpec((1, tile, GLA_HK), lambda b, h, t: (b, t, h))
    vspec = pl.BlockSpec((1, tile, GLA_HV), lambda b, h, t: (b, t, h))
    out = pl.pallas_call(
        functools.partial(_gla_core_kernel, tile=tile),
        grid=(bsz, GLA_HEADS, seq // tile),
        in_specs=[kspec, kspec, vspec, vspec, kspec,
                  pl.BlockSpec((1, 1, GLA_HV), lambda b, h, t: (h, 0, 0))],
        out_specs=vspec,
        out_shape=jax.ShapeDtypeStruct((bsz, seq, GLA_VAL_DIM), BF16),
        scratch_shapes=[pltpu.VMEM((GLA_HV, GLA_HK), F32)],
        compiler_params=_params(("parallel", "parallel", "arbitrary")),
        name="gla_core",
    )(q.reshape(bsz, seq, GLA_KEY_DIM), k.reshape(bsz, seq, GLA_KEY_DIM), v.reshape(bsz, seq, GLA_VAL_DIM),
      go.reshape(bsz, seq, GLA_VAL_DIM), la.reshape(bsz, seq, GLA_KEY_DIM),
      norm_gain.astype(F32).reshape(GLA_HEADS, 1, GLA_HV))
    return out.reshape(bsz * seq, GLA_VAL_DIM)


def _ffn(x, gf_ref, w1_ref, w2_ref):
    hb = _rms(x, gf_ref[...]).astype(BF16)
    d_ff = w1_ref.shape[1]
    step = 1024
    acc = x
    for c0 in range(0, d_ff, step):
        mid = jnp.maximum(_dot(hb, w1_ref[:, c0:c0 + step]), 0.0)
        acc = acc + _dot((mid * mid).astype(BF16), w2_ref[c0:c0 + step, :])
    return acc


def _mix_ffn_kernel(*refs, n_mix, nxt, final, tiles_per_seq):
    x_ref, y_refs = refs[0], refs[1:1 + n_mix]
    wo_ref, gf_ref, w1_ref, w2_ref, gl_ref = refs[1 + n_mix:6 + n_mix]
    n_nxt = 4 if nxt is not None else 0
    nxt_in = refs[6 + n_mix:6 + n_mix + n_nxt]
    o_ref = refs[6 + n_mix + n_nxt]
    nxt_out = refs[7 + n_mix + n_nxt:]
    mixed = y_refs[0][...] if n_mix == 1 else jnp.concatenate([r[...] for r in y_refs], axis=1)
    x = x_ref[...] + _dot(mixed, wo_ref[...])
    y = _ffn(x, gf_ref, w1_ref, w2_ref)
    o_ref[...] = _rms(y, gl_ref[...]) if final else y
    if nxt == "ab":
        _ab_in_body(y, pl.program_id(0) % tiles_per_seq == 0, *nxt_in, *nxt_out)
    elif nxt == "gla":
        _gla_in_body(y, *nxt_in, *nxt_out)


def _mix_ffn(x2, ys, w_out, w_out_layer, gain_ffn, w1, w2, ffn_layer, gain_last, final, bsz, seq, nxt=None,
             nxt_args=()):
    n, d = x2.shape
    tm = TOKEN_TILE
    tok = lambda width: pl.BlockSpec((tm, width), lambda i: (i, 0))
    in_specs = [tok(d)] + [tok(y.shape[1]) for y in ys] + [_layer_spec(w_out, w_out_layer), _const_spec((1, d)),
                                                           _layer_spec(w1, ffn_layer), _layer_spec(w2, ffn_layer),
                                                           _const_spec((1, d))]
    out_specs, out_shapes, scratch = [tok(d)], [jax.ShapeDtypeStruct((n, d), F32)], []
    if nxt == "ab":
        gain_n, w_in, pw_bd, pool_scale = nxt_args
        ni, no, ns, scratch = _ab_in_specs(n, d, tm, seq // tm, bsz, w_in, pw_bd)
        nxt_ops = (gain_n.reshape(1, d), w_in, pw_bd, pool_scale.astype(F32).reshape(1, POOL_WIDTH))
    elif nxt == "gla":
        gain_n, w_in_pad, w_up_pad, b_gk = nxt_args
        ni, no, ns = _gla_in_specs(n, d, tm, w_in_pad, w_up_pad)
        nxt_ops = (gain_n.reshape(1, d), w_in_pad, w_up_pad, b_gk.reshape(1, GLA_KEY_DIM))
    else:
        ni, no, ns, nxt_ops = [], [], [], ()
    return pl.pallas_call(
        functools.partial(_mix_ffn_kernel, n_mix=len(ys), nxt=nxt, final=final, tiles_per_seq=seq // tm),
        grid=(n // tm,),
        in_specs=in_specs + ni,
        out_specs=out_specs + no,
        out_shape=out_shapes + ns,
        scratch_shapes=scratch,
        compiler_params=_params(("arbitrary",)),
        name="mix_ffn",
    )(x2, *ys, w_out, gain_ffn.reshape(1, d), w1, w2, gain_last.reshape(1, d), *nxt_ops)


def _pool_block_diag(pool_w):
    g = POOL_GROUP_DIM
    z = jnp.zeros((g, g), pool_w.dtype)
    pairs = [jnp.block([[pool_w[2 * p], z], [z, pool_w[2 * p + 1]]]) for p in range(2)]
    return jnp.stack(pairs).astype(BF16)


def kernel(x, norm_mix, norm_ffn, norm_final, ab_w_in, ab_lambda, ab_subln, pool_w, pool_scale, ab_w_out, gla_w_in,
           gla_w_gk_up, gla_b_gk, gla_norm, gla_w_out, ffn_w1, ffn_w2):
    bsz, seq, d = x.shape
    depth = norm_mix.shape[0]
    x2 = x.reshape(bsz * seq, d).astype(F32)
    w1_all, w2_all = ffn_w1.astype(BF16), ffn_w2.astype(BF16)
    ab_wo_all, gla_wo_all = ab_w_out.astype(BF16), gla_w_out.astype(BF16)

    def in_args(i):
        if i % 2 == 0:
            e = i // 2
            return "ab", (norm_mix[i], ab_w_in[e].astype(BF16), _pool_block_diag(pool_w[e]), pool_scale[e])
        o = i // 2
        w_in = jnp.pad(gla_w_in[o], ((0, 0), (0, GLA_GATE_PAD - GLA_GATE_RANK))).astype(BF16)
        assert w_in.shape[1] == 2 * GLA_KEY_DIM + 2 * GLA_VAL_DIM + GLA_GATE_PAD
        w_up = jnp.pad(gla_w_gk_up[o], ((0, GLA_GATE_PAD - GLA_GATE_RANK), (0, 0))).astype(BF16)
        return "gla", (norm_mix[i], w_in, w_up, gla_b_gk[o])

    mix_in = _ab_in_proj(x2, *in_args(0)[1], bsz, seq)
    for i in range(depth):
        final = i == depth - 1
        nxt, nxt_args = (None, ()) if final else in_args(i + 1)
        if i % 2 == 0:
            e = i // 2
            q, k, vt, b = mix_in
            a = _diff_attention(q, k, vt, ab_lambda[e], ab_subln[e], i, bsz, seq)
            ys, wo_all, wo_layer = [a, b], ab_wo_all, e
        else:
            o = i // 2
            q, k, v, go, la = mix_in
            ys, wo_all, wo_layer = [_gla_core(q, k, v, go, la, gla_norm[o], bsz, seq)], gla_wo_all, o
        outs = _mix_ffn(x2, ys, wo_all, wo_layer, norm_ffn[i], w1_all, w2_all, i, norm_final, final, bsz, seq, nxt,
                        nxt_args)
        x2, mix_in = outs[0], outs[1:]
    return x2.reshape(bsz, seq, d)
```
